```python
import jax, jax.numpy as jnp
from jax import lax
import numpy as np

D_MODEL = 2048
BATCH = 2
SEQ = 8192
DEPTH = 2
DEC_BATCH = 32
DEC_SEQ = 16
PAST_LEN = 2048

CHUNK = 64
QBLK = 128
ROPE_THETA = 10000.0
EPS = 1e-6

GLA_HEADS = 4
GLA_DK = 128
GLA_DV = 256
GLA_RANK = 16
GLA_TAU = 16.0
DSA_HEADS = 8
DSA_KV = 2
DSA_GROUP = DSA_HEADS // DSA_KV
HEAD_DIM = 128
IDX_HEADS = 8
IDX_DIM = 64
TOPK_MAX = 256
MIX_A = GLA_HEADS * GLA_DV
MIX_B = DSA_HEADS * HEAD_DIM
D_FF = 5632
N_EXPERTS = 8
TOP_K = 2
N_DENSE = (DEPTH + 1) // 2
N_MOE = DEPTH // 2

COL_SIZES = (
    ('gla_q', GLA_HEADS * GLA_DK),
    ('gla_k', GLA_HEADS * GLA_DK),
    ('gla_v', GLA_HEADS * GLA_DV),
    ('gla_glr', GLA_RANK),
    ('gla_r', GLA_HEADS * GLA_DV),
    ('dsa_q', DSA_HEADS * HEAD_DIM),
    ('dsa_k', DSA_KV * HEAD_DIM),
    ('dsa_v', DSA_KV * HEAD_DIM),
    ('idx_q', IDX_HEADS * IDX_DIM),
    ('idx_k', IDX_DIM),
    ('idx_w', IDX_HEADS),
    ('gate_a', D_MODEL),
    ('gate_b', D_MODEL),
)
COL_NAMES = tuple(n for n, _ in COL_SIZES)
COL_SPLITS = tuple(int(s) for s in np.cumsum([c for _, c in COL_SIZES])[:-1])
P_TOTAL = sum(c for _, c in COL_SIZES)

kernel_name = 'hybrid_gla_dsa_stream_step'


def rms_norm(x, g):
    xf = x.astype(jnp.float32)
    y = xf * lax.rsqrt(jnp.mean(xf * xf, axis=-1, keepdims=True) + EPS)
    return (y * g.astype(jnp.float32)).astype(x.dtype)


def rope(x, pos):
    half = x.shape[-1] // 2
    inv = ROPE_THETA ** (-jnp.arange(half, dtype=jnp.float32) / half)
    ang = pos.astype(jnp.float32)[:, None] * inv[None, :]
    cos = jnp.cos(ang)[None, :, None, :]
    sin = jnp.sin(ang)[None, :, None, :]
    xf = x.astype(jnp.float32)
    x1, x2 = xf[..., :half], xf[..., half:]
    return jnp.concatenate([x1 * cos - x2 * sin, x2 * cos + x1 * sin], axis=-1).astype(x.dtype)


def gla_chunk(S, q, k, v, g):
    b = jnp.cumsum(g, axis=2)
    qe = q * jnp.exp(b)
    ke = k * jnp.exp(-b)
    C = q.shape[2]
    causal = jnp.tril(jnp.ones((C, C), dtype=bool))
    A = jnp.where(causal, jnp.einsum('bhik,bhjk->bhij', qe, ke), 0.0)
    o = jnp.einsum('bhij,bhjv->bhiv', A, v) + jnp.einsum('bhik,bhkv->bhiv', qe, S)
    b_last = b[:, :, -1:, :]
    S_new = jnp.exp(b_last[:, :, 0, :])[..., None] * S + jnp.einsum('bhjk,bhjv->bhkv', k * jnp.exp(b_last - b), v)
    return S_new, o


def gla_full(q, k, v, g):
    B, H, T, _ = q.shape
    n = T // CHUNK

    def blocks(a):
        return jnp.moveaxis(a.reshape(B, H, n, CHUNK, a.shape[-1]), 2, 0)

    S0 = jnp.zeros((B, H, GLA_DK, GLA_DV), jnp.float32)
    S, o = lax.scan(lambda s, xs: gla_chunk(s, *xs), S0, (blocks(q), blocks(k), blocks(v), blocks(g)))
    return S, jnp.moveaxis(o, 0, 2).reshape(B, H, T, GLA_DV)


def gla_branch(p, w_g2, b_g, g_out, S0):
    B, T, _ = p['gla_q'].shape

    def heads(a, d):
        return a.reshape(B, T, GLA_HEADS, d).transpose(0, 2, 1, 3).astype(jnp.float32)

    q = heads(p['gla_q'], GLA_DK) * (GLA_DK ** -0.5)
    k = heads(p['gla_k'], GLA_DK)
    v = heads(p['gla_v'], GLA_DV)
    z = p['gla_glr'] @ w_g2 + b_g
    g = heads(jax.nn.log_sigmoid(z.astype(jnp.float32)) / GLA_TAU, GLA_DK)
    if S0 is None:
        S, o = gla_full(q, k, v, g)
    else:
        S, o = gla_chunk(S0.astype(jnp.float32), q, k, v, g)
    o = rms_norm(o.transpose(0, 2, 1, 3), g_out)
    r = p['gla_r'].reshape(B, T, GLA_HEADS, GLA_DV).astype(jnp.float32)
    o = (jax.nn.silu(r) * o).reshape(B, T, MIX_A)
    return o.astype(p['gla_q'].dtype), S


def dsa_prep(p, g_q, g_k, g_ki, pos):
    B, T, _ = p['dsa_q'].shape
    q = rope(rms_norm(p['dsa_q'].reshape(B, T, DSA_HEADS, HEAD_DIM), g_q), pos)
    k = rope(rms_norm(p['dsa_k'].reshape(B, T, DSA_KV, HEAD_DIM), g_k), pos)
    v = p['dsa_v'].reshape(B, T, DSA_KV, HEAD_DIM)
    qi = rope(p['idx_q'].reshape(B, T, IDX_HEADS, IDX_DIM), pos)
    ki = rope(rms_norm(p['idx_k'], g_ki)[:, :, None, :], pos)[:, :, 0, :]
    return q, k, v, qi, ki, p['idx_w']


def dsa_attend(q, qi, wi, q_pos, k, v, ki, n_sel):
    B, Tq = q.shape[:2]
    L = k.shape[1]
    k_pos = jnp.arange(L, dtype=jnp.int32)
    adm = (k_pos[None, :] // CHUNK) <= (q_pos[:, None] // CHUNK)
    s = jnp.einsum('bthd,bsd->bths', qi.astype(jnp.float32), ki.astype(jnp.float32)) * (IDX_DIM ** -0.5)
    I = jnp.einsum('bths,bth->bts', jax.nn.relu(s), wi.astype(jnp.float32)) * (IDX_HEADS ** -0.5)
    I = jnp.where(adm[None], I, -jnp.inf)
    top, idx = lax.top_k(I, n_sel)
    valid = top > -jnp.inf
    k_sel = jax.vmap(lambda a, i: a[i])(k, idx)
    v_sel = jax.vmap(lambda a, i: a[i])(v, idx)
    qg = q.reshape(B, Tq, DSA_KV, DSA_GROUP, HEAD_DIM).astype(jnp.float32)
    logits = jnp.einsum('btkgd,btskd->btkgs', qg, k_sel.astype(jnp.float32)) * (HEAD_DIM ** -0.5)
    logits = jnp.where(valid[:, :, None, None, :], logits, -jnp.inf)
    pr = jax.nn.softmax(logits, axis=-1)
    o = jnp.einsum('btkgs,btskd->btkgd', pr, v_sel.astype(jnp.float32))
    return o.reshape(B, Tq, MIX_B).astype(q.dtype)


def dsa_full(q, qi, wi, k, v, ki, pos, n_sel):
    B, T = q.shape[:2]
    nblk = T // QBLK

    def to_blocks(a):
        return jnp.moveaxis(a.reshape((B, nblk, QBLK) + a.shape[2:]), 1, 0)

    def one_block(args):
        qb, qib, wib, pb = args
        return dsa_attend(qb, qib, wib, pb, k, v, ki, n_sel)

    o = lax.map(one_block, (to_blocks(q), to_blocks(qi), to_blocks(wi), pos.reshape(nblk, QBLK)))
    return jnp.moveaxis(o, 0, 1).reshape(B, T, MIX_B)


def token_mixer(x, pos, cache, g_attn, w_in, w_g2, b_g, g_gla, g_q, g_k, g_ki, w_pa, w_pb, w_o):
    xn = rms_norm(x, g_attn)
    p = dict(zip(COL_NAMES, jnp.split(xn @ w_in, COL_SPLITS, axis=-1)))
    S0 = None if cache is None else cache[0]
    o_a, S = gla_branch(p, w_g2, b_g, g_gla, S0)
    q, k, v, qi, ki, wi = dsa_prep(p, g_q, g_k, g_ki, pos)
    if cache is None:
        n_sel = min(TOPK_MAX, x.shape[1] // 4)
        o_b = dsa_full(q, qi, wi, k, v, ki, pos, n_sel)
    else:
        ck, cv, cki = cache[1], cache[2], cache[3]
        k_all = jnp.concatenate([ck.astype(k.dtype), k], axis=1)
        v_all = jnp.concatenate([cv.astype(v.dtype), v], axis=1)
        ki_all = jnp.concatenate([cki.astype(ki.dtype), ki], axis=1)
        n_sel = min(TOPK_MAX, k_all.shape[1] // 4)
        o_b = dsa_attend(q, qi, wi, pos, k_all, v_all, ki_all, n_sel)
    merged = jax.nn.sigmoid(p['gate_a']) * (o_a @ w_pa) + jax.nn.sigmoid(p['gate_b']) * (o_b @ w_pb)
    return merged @ w_o, S.astype(x.dtype), k, v, ki


def swiglu(x, wg, wu, wd):
    return (jax.nn.silu(x @ wg) * (x @ wu)) @ wd


def moe_swiglu(x, w_r, wg, wu, wd):
    B, T, D = x.shape
    xf = x.reshape(B * T, D)
    logits = (xf @ w_r).astype(jnp.float32)
    top, idx = lax.top_k(logits, TOP_K)
    wts = jax.nn.softmax(top, axis=-1)
    gate = jnp.sum(jax.nn.one_hot(idx, N_EXPERTS, dtype=jnp.float32) * wts[..., None], axis=1)
    out = jnp.zeros((B * T, D), jnp.float32)
    for e in range(N_EXPERTS):
        out = out + gate[:, e:e + 1] * swiglu(xf, wg[e], wu[e], wd[e]).astype(jnp.float32)
    return out.astype(x.dtype).reshape(B, T, D)


def channel_mixer(h, layer, g, w_ff_gate, w_ff_up, w_ff_down, w_router, w_moe_gate, w_moe_up, w_moe_down):
    hn = rms_norm(h, g)
    j = layer // 2
    if layer % 2 == 0:
        return swiglu(hn, w_ff_gate[j], w_ff_up[j], w_ff_down[j])
    return moe_swiglu(hn, w_router[j], w_moe_gate[j], w_moe_up[j], w_moe_down[j])


def setup_inputs(seed: int = 0) -> dict:
    key = jax.random.key(seed)
    ks = jax.random.split(key, 32)

    def nrm(k, shape, scale):
        return jax.random.normal(k, shape, jnp.float32) * scale

    def gain(k, shape):
        return 1.0 + 0.02 * jax.random.normal(k, shape, jnp.float32)

    return {
        'x_prompt': nrm(ks[0], (BATCH, SEQ, D_MODEL), 1.0),
        'x_sample': nrm(ks[1], (DEC_BATCH, DEC_SEQ, D_MODEL), 1.0),
        'cache_k': nrm(ks[2], (DEPTH, DEC_BATCH, PAST_LEN, DSA_KV, HEAD_DIM), 1.0),
        'cache_v': nrm(ks[3], (DEPTH, DEC_BATCH, PAST_LEN, DSA_KV, HEAD_DIM), 1.0),
        'cache_ki': nrm(ks[4], (DEPTH, DEC_BATCH, PAST_LEN, IDX_DIM), 1.0),
        'state_gla': nrm(ks[5], (DEPTH, DEC_BATCH, GLA_HEADS, GLA_DK, GLA_DV), 0.1),
        'g_attn': gain(ks[6], (DEPTH, D_MODEL)),
        'w_in': nrm(ks[7], (DEPTH, D_MODEL, P_TOTAL), D_MODEL ** -0.5),
        'w_gla_gate2': nrm(ks[8], (DEPTH, GLA_RANK, GLA_HEADS * GLA_DK), GLA_RANK ** -0.5),
        'b_gla_gate': nrm(ks[9], (DEPTH, GLA_HEADS * GLA_DK), 0.1),
        'g_gla_out': gain(ks[10], (DEPTH, GLA_DV)),
        'g_q': gain(ks[11], (DEPTH, HEAD_DIM)),
        'g_k': gain(ks[12], (DEPTH, HEAD_DIM)),
        'g_ki': gain(ks[13], (DEPTH, IDX_DIM)),
        'w_branch_a': nrm(ks[14], (DEPTH, MIX_A, D_MODEL), MIX_A ** -0.5),
        'w_branch_b': nrm(ks[15], (DEPTH, MIX_B, D_MODEL), MIX_B ** -0.5),
        'w_out': nrm(ks[16], (DEPTH, D_MODEL, D_MODEL), D_MODEL ** -0.5),
        'g_ffn': gain(ks[17], (DEPTH, D_MODEL)),
        'w_ff_gate': nrm(ks[18], (N_DENSE, D_MODEL, D_FF), D_MODEL ** -0.5),
        'w_ff_up': nrm(ks[19], (N_DENSE, D_MODEL, D_FF), D_MODEL ** -0.5),
        'w_ff_down': nrm(ks[20], (N_DENSE, D_FF, D_MODEL), D_FF ** -0.5),
        'w_router': nrm(ks[21], (N_MOE, D_MODEL, N_EXPERTS), D_MODEL ** -0.5),
        'w_moe_gate': nrm(ks[22], (N_MOE, N_EXPERTS, D_MODEL, D_FF), D_MODEL ** -0.5),
        'w_moe_up': nrm(ks[23], (N_MOE, N_EXPERTS, D_MODEL, D_FF), D_MODEL ** -0.5),
        'w_moe_down': nrm(ks[24], (N_MOE, N_EXPERTS, D_FF, D_MODEL), D_FF ** -0.5),
    }


def reference(x_prompt, x_sample, cache_k, cache_v, cache_ki, state_gla, g_attn, w_in, w_gla_gate2, b_gla_gate,
              g_gla_out, g_q, g_k, g_ki, w_branch_a, w_branch_b, w_out, g_ffn, w_ff_gate, w_ff_up, w_ff_down,
              w_router, w_moe_gate, w_moe_up, w_moe_down):
    pos_p = jnp.arange(SEQ, dtype=jnp.int32)
    pos_s = PAST_LEN + jnp.arange(DEC_SEQ, dtype=jnp.int32)
    hp, hs = x_prompt, x_sample
    kp_l, vp_l, kip_l, sp_l = [], [], [], []
    ks_l, vs_l, kis_l, ss_l = [], [], [], []
    for l in range(DEPTH):
        mix_w = (g_attn[l], w_in[l], w_gla_gate2[l], b_gla_gate[l], g_gla_out[l], g_q[l], g_k[l], g_ki[l],
                 w_branch_a[l], w_branch_b[l], w_out[l])
        m_p, s_p, k_p, v_p, ki_p = token_mixer(hp, pos_p, None, *mix_w)
        m_s, s_s, k_s, v_s, ki_s = token_mixer(hs, pos_s, (state_gla[l], cache_k[l], cache_v[l], cache_ki[l]), *mix_w)
        hp = hp + m_p
        hs = hs + m_s
        hp = hp + channel_mixer(hp, l, g_ffn[l], w_ff_gate, w_ff_up, w_ff_down, w_router, w_moe_gate, w_moe_up, w_moe_down)
        hs = hs + channel_mixer(hs, l, g_ffn[l], w_ff_gate, w_ff_up, w_ff_down, w_router, w_moe_gate, w_moe_up, w_moe_down)
        kp_l.append(k_p); vp_l.append(v_p); kip_l.append(ki_p); sp_l.append(s_p)
        ks_l.append(k_s); vs_l.append(v_s); kis_l.append(ki_s); ss_l.append(s_s)
    k_prompt = jnp.stack(kp_l)
    v_prompt = jnp.stack(vp_l)
    ki_prompt = jnp.stack(kip_l)
    gla_prompt = jnp.stack(sp_l)
    k_sample = jnp.stack(ks_l)
    v_sample = jnp.stack(vs_l)
    ki_sample = jnp.stack(kis_l)
    gla_sample = jnp.stack(ss_l)
    return (hp, hs, k_prompt, v_prompt, ki_prompt, gla_prompt, k_sample, v_sample, ki_sample, gla_sample)
```

```python
import functools

import jax
import jax.numpy as jnp
from jax import lax
from jax.experimental import pallas as pl
from jax.experimental.pallas import tpu as pltpu

F32 = jnp.float32
BF16 = jnp.bfloat16
I32 = jnp.int32

D_MODEL = 2048
BATCH = 2
SEQ = 8192
DEPTH = 2
DEC_BATCH = 32
DEC_SEQ = 16
PAST_LEN = 2048
CHUNK = 64
QBLK = 128
ROPE_THETA = 10000.0
EPS = 1e-6
GLA_HEADS = 4
GLA_DK = 128
GLA_DV = 256
GLA_RANK = 16
GLA_TAU = 16.0
DSA_HEADS = 8
DSA_KV = 2
DSA_GROUP = DSA_HEADS // DSA_KV
HEAD_DIM = 128
IDX_HEADS = 8
IDX_DIM = 64
TOPK_MAX = 256
MIX_A = GLA_HEADS * GLA_DV
MIX_B = DSA_HEADS * HEAD_DIM
D_FF = 5632
N_EXPERTS = 8
TOP_K = 2

N_PROMPT = BATCH * SEQ
N_SAMPLE = DEC_BATCH * DEC_SEQ
N_TOK = N_PROMPT + N_SAMPLE

LANES = 128
VMEM_LIMIT_BYTES = 56 * 1024 * 1024

COL_SIZES = (
    ('gla_q', 512), ('gla_k', 512), ('gla_v', 1024), ('gla_glr', 16), ('gla_r', 1024),
    ('dsa_q', 1024), ('dsa_k', 256), ('dsa_v', 256), ('idx_q', 512), ('idx_k', 64),
    ('idx_w', 8), ('gate_a', 2048), ('gate_b', 2048),
)
PERM_ORDER = ('gla_q', 'gla_k', 'gla_v', 'gla_r', 'dsa_q', 'dsa_k', 'dsa_v', 'idx_q',
              'gate_a', 'gate_b', 'idx_k', 'gla_glr', 'idx_w')
P_COLS = 9728
OFF_GLA_Q, OFF_GLA_K, OFF_GLA_V, OFF_GLA_R = 0, 512, 1024, 2048
OFF_DSA_Q, OFF_DSA_K, OFF_DSA_V, OFF_IDX_Q = 3072, 4096, 4352, 4608
OFF_GATE_A, OFF_GATE_B, OFF_SMALL = 5120, 7168, 9216
SM_IDX_K, SM_GLR, SM_IDX_W = 0, 64, 80

KEY_NEG_INF = -2139095041
INT_MIN = -2147483648
MASK_BIAS = -1e30
M_INIT = -5e29


def _cparams(*sem):
    return pltpu.CompilerParams(dimension_semantics=sem, vmem_limit_bytes=VMEM_LIMIT_BYTES)


def _dot(a, b):
    return jnp.dot(a, b, preferred_element_type=F32)


def _dot_nt(a, b):
    return lax.dot_general(a, b, (((1,), (1,)), ((), ())), preferred_element_type=F32)


def _dot_tn(a, b):
    return lax.dot_general(a, b, (((0,), (0,)), ((), ())), preferred_element_type=F32)


def _split3(a):
    a1 = a.astype(BF16)
    r1 = a - a1.astype(F32)
    a2 = r1.astype(BF16)
    a3 = (r1 - a2.astype(F32)).astype(BF16)
    return a1, a2, a3


def _rmsnorm_kernel(x_ref, g_ref, o_ref):
    x = x_ref[...]
    ms = jnp.mean(x * x, axis=-1, keepdims=True)
    o_ref[...] = ((x * lax.rsqrt(ms + EPS)) * g_ref[...]).astype(BF16)


def rmsnorm_cast(x, g, tm):
    n, d = x.shape
    return pl.pallas_call(
        _rmsnorm_kernel,
        grid=(n // tm,),
        in_specs=[pl.BlockSpec((tm, d), lambda i: (i, 0)), pl.BlockSpec((1, d), lambda i: (0, 0))],
        out_specs=pl.BlockSpec((tm, d), lambda i: (i, 0)),
        out_shape=jax.ShapeDtypeStruct((n, d), BF16),
        compiler_params=_cparams("parallel"),
        name="rmsnorm_cast",
    )(x, g.reshape(1, d))


def _mm_kernel(x_ref, w_ref, o_ref):
    o_ref[...] = _dot(x_ref[...], w_ref[...].astype(BF16))


def _mm_res_kernel(x_ref, w_ref, r_ref, o_ref):
    o_ref[...] = r_ref[...] + _dot(x_ref[...], w_ref[...].astype(BF16))


def matmul(x, w, tm, tn, residual=None, name="matmul"):
    m, k = x.shape
    n = w.shape[1]
    in_specs = [pl.BlockSpec((tm, k), lambda i, j: (i, 0)), pl.BlockSpec((k, tn), lambda i, j: (0, j))]
    args = [x, w]
    body = _mm_kernel
    if residual is not None:
        in_specs.append(pl.BlockSpec((tm, tn), lambda i, j: (i, j)))
        args.append(residual)
        body = _mm_res_kernel
    return pl.pallas_call(
        body,
        grid=(m // tm, n // tn),
        in_specs=in_specs,
        out_specs=pl.BlockSpec((tm, tn), lambda i, j: (i, j)),
        out_shape=jax.ShapeDtypeStruct((m, n), F32),
        compiler_params=_cparams("parallel", "parallel"),
        name=name,
    )(*args)


def _merge_kernel(oa_ref, ob_ref, wa_ref, wb_ref, ga_ref, gb_ref, o_ref):
    a = _dot(oa_ref[...], wa_ref[...].astype(BF16))
    b = _dot(ob_ref[...], wb_ref[...].astype(BF16))
    o_ref[...] = (jax.nn.sigmoid(ga_ref[...]) * a + jax.nn.sigmoid(gb_ref[...]) * b).astype(BF16)


def merge_branches(o_a, o_b, w_pa, w_pb, proj, tm, tn):
    m = o_a.shape[0]
    n = w_pa.shape[1]
    ja, jb = OFF_GATE_A // tn, OFF_GATE_B // tn
    return pl.pallas_call(
        _merge_kernel,
        grid=(m // tm, n // tn),
        in_specs=[
            pl.BlockSpec((tm, MIX_A), lambda i, j: (i, 0)),
            pl.BlockSpec((tm, MIX_B), lambda i, j: (i, 0)),
            pl.BlockSpec((MIX_A, tn), lambda i, j: (0, j)),
            pl.BlockSpec((MIX_B, tn), lambda i, j: (0, j)),
            pl.BlockSpec((tm, tn), lambda i, j: (i, ja + j)),
            pl.BlockSpec((tm, tn), lambda i, j: (i, jb + j)),
        ],
        out_specs=pl.BlockSpec((tm, tn), lambda i, j: (i, j)),
        out_shape=jax.ShapeDtypeStruct((m, n), BF16),
        compiler_params=_cparams("parallel", "parallel"),
        name="merge_branches",
    )(o_a, o_b, w_pa, w_pb, proj, proj)


def _swiglu_kernel(te_ref, x_ref, wg_ref, wu_ref, o_ref):
    del te_ref
    x = x_ref[...]
    a = _dot(x, wg_ref[...].astype(BF16))
    b = _dot(x, wu_ref[...].astype(BF16))
    o_ref[...] = (a * jax.nn.sigmoid(a) * b).astype(BF16)


def swiglu_up(x, w_gate, w_up, tile_expert, tm, tn):
    m, k = x.shape
    f = w_gate.shape[2]
    grid_spec = pltpu.PrefetchScalarGridSpec(
        num_scalar_prefetch=1,
        grid=(m // tm, f // tn),
        in_specs=[
            pl.BlockSpec((tm, k), lambda i, j, te: (i, 0)),
            pl.BlockSpec((None, k, tn), lambda i, j, te: (te[i], 0, j)),
            pl.BlockSpec((None, k, tn), lambda i, j, te: (te[i], 0, j)),
        ],
        out_specs=pl.BlockSpec((tm, tn), lambda i, j, te: (i, j)),
    )
    return pl.pallas_call(
        _swiglu_kernel,
        grid_spec=grid_spec,
        out_shape=jax.ShapeDtypeStruct((m, f), BF16),
        compiler_params=_cparams("parallel", "parallel"),
        name="swiglu_up",
    )(tile_expert, x, w_gate, w_up)


def _down_kernel(te_ref, a_ref, w_ref, r_ref, o_ref):
    del te_ref
    kf = pl.program_id(1)

    @pl.when(kf == 0)
    def _():
        o_ref[...] = r_ref[...]

    o_ref[...] += _dot(a_ref[...], w_ref[...].astype(BF16))


def _down_gated_kernel(te_ref, a_ref, w_ref, r_ref, gate_ref, o_ref, acc_ref):
    del te_ref
    kf = pl.program_id(1)

    @pl.when(kf == 0)
    def _():
        acc_ref[...] = jnp.zeros_like(acc_ref)

    acc_ref[...] += _dot(a_ref[...], w_ref[...].astype(BF16))

    @pl.when(kf == pl.num_programs(1) - 1)
    def _():
        o_ref[...] = r_ref[...] + gate_ref[...] * acc_ref[...]


def swiglu_down(a, w_down, tile_expert, residual, tm, tk, gate=None):
    m, f = a.shape
    d = w_down.shape[2]
    in_specs = [
        pl.BlockSpec((tm, tk), lambda i, kf, te: (i, kf)),
        pl.BlockSpec((None, tk, d), lambda i, kf, te: (te[i], kf, 0)),
        pl.BlockSpec((tm, d), lambda i, kf, te: (i, 0)),
    ]
    args = [tile_expert, a, w_down, residual]
    scratch = []
    body = _down_kernel
    if gate is not None:
        in_specs.append(pl.BlockSpec((tm, 1), lambda i, kf, te: (i, 0)))
        args.append(gate)
        scratch = [pltpu.VMEM((tm, d), F32)]
        body = _down_gated_kernel
    grid_spec = pltpu.PrefetchScalarGridSpec(
        num_scalar_prefetch=1,
        grid=(m // tm, f // tk),
        in_specs=in_specs,
        out_specs=pl.BlockSpec((tm, d), lambda i, kf, te: (i, 0)),
        scratch_shapes=scratch,
    )
    return pl.pallas_call(
        body,
        grid_spec=grid_spec,
        out_shape=jax.ShapeDtypeStruct((m, d), F32),
        compiler_params=_cparams("parallel", "arbitrary"),
        name="swiglu_down",
    )(*args)


def _router_kernel(x_ref, w_ref, o_ref):
    logits = _dot(x_ref[...], w_ref[...].astype(BF16))
    lane = lax.broadcasted_iota(I32, logits.shape, 1)
    logits = jnp.where(lane < N_EXPERTS, logits, -jnp.inf)
    t1 = jnp.max(logits, axis=-1, keepdims=True)
    i1 = jnp.min(jnp.where(logits == t1, lane, LANES), axis=-1, keepdims=True)
    rest = jnp.where(lane == i1, -jnp.inf, logits)
    t2 = jnp.max(rest, axis=-1, keepdims=True)
    i2 = jnp.min(jnp.where(rest == t2, lane, LANES), axis=-1, keepdims=True)
    e2 = jnp.exp(t2 - t1)
    den = 1.0 + e2
    o_ref[...] = jnp.where(lane == i1, 1.0 / den, 0.0) + jnp.where(lane == i2, e2 / den, 0.0)


def moe_router(x, w_router, tm):
    m, k = x.shape
    w_pad = jnp.pad(w_router, ((0, 0), (0, LANES - w_router.shape[1])))
    return pl.pallas_call(
        _router_kernel,
        grid=(m // tm,),
        in_specs=[pl.BlockSpec((tm, k), lambda i: (i, 0)), pl.BlockSpec((k, LANES), lambda i: (0, 0))],
        out_specs=pl.BlockSpec((tm, LANES), lambda i: (i, 0)),
        out_shape=jax.ShapeDtypeStruct((m, LANES), F32),
        compiler_params=_cparams("parallel"),
        name="moe_router",
    )(x, w_pad)


def _prep_kernel(q_ref, k_ref, v_ref, qi_ref, sm_ref, c128_ref, s128_ref, c64_ref, s64_ref,
                 gq_ref, gk_ref, gki_ref,
                 qo_ref, kf_ref, kb_ref, vb_ref, qio_ref, kif_ref, kib_ref):
    c128, s128 = c128_ref[...], s128_ref[...]
    c64, s64 = c64_ref[...], s64_ref[...]
    lane = lax.broadcasted_iota(I32, c64.shape, 1)
    lane_lo = (lane & 63) < 32

    def rope128(x):
        return x * c128 + pltpu.roll(x, 64, 1) * s128

    def rope64(x):
        partner = jnp.where(lane_lo, pltpu.roll(x, 96, 1), pltpu.roll(x, 32, 1))
        return x * c64 + partner * s64

    def norm128(x, g):
        ms = jnp.mean(x * x, axis=-1, keepdims=True)
        return x * lax.rsqrt(ms + EPS) * g

    gq, gk = gq_ref[...], gk_ref[...]
    for h in range(DSA_HEADS):
        sl = slice(h * HEAD_DIM, (h + 1) * HEAD_DIM)
        qo_ref[:, sl] = (rope128(norm128(q_ref[:, sl], gq)) * (HEAD_DIM ** -0.5)).astype(BF16)
    for h in range(DSA_KV):
        sl = slice(h * HEAD_DIM, (h + 1) * HEAD_DIM)
        kr = rope128(norm128(k_ref[:, sl], gk))
        kf_ref[:, sl] = kr
        kb_ref[:, sl] = kr.astype(BF16)
    vb_ref[...] = v_ref[...].astype(BF16)
    for t in range(IDX_HEADS * IDX_DIM // LANES):
        sl = slice(t * LANES, (t + 1) * LANES)
        qio_ref[:, sl] = rope64(qi_ref[:, sl]).astype(BF16)
    sm = sm_ref[...]
    ms = jnp.sum(jnp.where(lane < IDX_DIM, sm * sm, 0.0), axis=-1, keepdims=True) * (1.0 / IDX_DIM)
    ki = rope64(sm * lax.rsqrt(ms + EPS) * gki_ref[...])
    kif_ref[...] = ki[:, :IDX_DIM]
    kib_ref[...] = ki[:, :IDX_DIM].astype(BF16)


def dsa_prep(proj, tabs, g_q, g_k, g_ki, tm):
    n = proj.shape[0]
    c128, s128, c64, s64 = tabs
    gki_pad = jnp.pad(g_ki, (0, LANES - IDX_DIM)).reshape(1, LANES)
    row = lambda w, off: pl.BlockSpec((tm, w), lambda i: (i, off // w))
    tab = pl.BlockSpec((tm, LANES), lambda i: (i, 0))
    vec = pl.BlockSpec((1, LANES), lambda i: (0, 0))
    out = lambda w: pl.BlockSpec((tm, w), lambda i: (i, 0))
    kvw = DSA_KV * HEAD_DIM
    return pl.pallas_call(
        _prep_kernel,
        grid=(n // tm,),
        in_specs=[row(MIX_B, OFF_DSA_Q), row(kvw, OFF_DSA_K), row(kvw, OFF_DSA_V),
                  row(IDX_HEADS * IDX_DIM, OFF_IDX_Q), row(LANES, OFF_SMALL),
                  tab, tab, tab, tab, vec, vec, vec],
        out_specs=[out(MIX_B), out(kvw), out(kvw), out(kvw), out(IDX_HEADS * IDX_DIM),
                   out(IDX_DIM), out(IDX_DIM)],
        out_shape=[
            jax.ShapeDtypeStruct((n, MIX_B), BF16),
            jax.ShapeDtypeStruct((n, kvw), F32),
            jax.ShapeDtypeStruct((n, kvw), BF16),
            jax.ShapeDtypeStruct((n, kvw), BF16),
            jax.ShapeDtypeStruct((n, IDX_HEADS * IDX_DIM), BF16),
            jax.ShapeDtypeStruct((n, IDX_DIM), F32),
            jax.ShapeDtypeStruct((n, IDX_DIM), BF16),
        ],
        compiler_params=_cparams("parallel"),
        name="dsa_prep",
    )(proj, proj, proj, proj, proj, c128, s128, c64, s64,
      g_q.reshape(1, LANES), g_k.reshape(1, LANES), gki_pad)


def rope_tables(pos):
    def tab(half, reps):
        inv = ROPE_THETA ** (-jnp.arange(half, dtype=F32) / half)
        ang = pos.astype(F32)[:, None] * inv[None, :]
        cos, sin = jnp.cos(ang), jnp.sin(ang)
        return jnp.tile(jnp.concatenate([cos, cos], -1), (1, reps)), jnp.tile(jnp.concatenate([-sin, sin], -1), (1, reps))
    c128, s128 = tab(HEAD_DIM // 2, 1)
    c64, s64 = tab(IDX_DIM // 2, 2)
    return c128, s128, c64, s64


def _gla_kernel(q_ref, k_ref, v_ref, r_ref, sm_ref, wg2_ref, bg_ref, gout_ref, s0_ref,
                o_ref, sout_ref, st_ref, *, chunk, n_chunks):
    t = pl.program_id(2)

    @pl.when(t == 0)
    def _():
        st_ref[...] = s0_ref[...].T

    row = lax.broadcasted_iota(I32, (chunk, chunk), 0)
    col = lax.broadcasted_iota(I32, (chunk, chunk), 1)
    causal = row >= col
    tril = jnp.where(causal, 1.0, 0.0).astype(BF16)
    w1, w2, w3 = _split3(wg2_ref[...])
    bg = bg_ref[...]
    gout = gout_ref[...]

    for c in range(n_chunks):
        sl = pl.ds(c * chunk, chunk)
        q = q_ref[sl, :] * (GLA_DK ** -0.5)
        k = k_ref[sl, :]
        v = v_ref[sl, :].astype(BF16)
        s1, s2, s3 = _split3(sm_ref[sl, :])
        z = (_dot(s1, w1) + (_dot(s1, w2) + _dot(s2, w1)) + (_dot(s1, w3) + _dot(s2, w2) + _dot(s3, w1))) + bg
        g = (jnp.minimum(z, 0.0) - jnp.log1p(jnp.exp(-jnp.abs(z)))) * (1.0 / GLA_TAU)
        g1, g2, g3 = _split3(g)
        b = _dot(tril, g1) + _dot(tril, g2) + _dot(tril, g3)
        b_last = b[chunk - 1:chunk, :]
        qe = (q * jnp.exp(b)).astype(BF16)
        ke = (k * jnp.exp(-b)).astype(BF16)
        kd = (k * jnp.exp(b_last - b)).astype(BF16)
        a = jnp.where(causal, _dot_nt(qe, ke), 0.0).astype(BF16)
        st = st_ref[...]
        o = _dot(a, v) + _dot_nt(qe, st.astype(BF16))
        st_ref[...] = jnp.exp(b_last) * st + _dot_tn(v, kd)
        ms = jnp.mean(o * o, axis=-1, keepdims=True)
        on = o * lax.rsqrt(ms + EPS) * gout
        r = r_ref[sl, :]
        o_ref[sl, :] = (r * jax.nn.sigmoid(r) * on).astype(BF16)

    @pl.when(t == pl.num_programs(2) - 1)
    def _():
        sout_ref[...] = st_ref[...].T


def gla_scan(proj, wg2_pad, b_g, g_out, s0, row0, seq, chunk, tc):
    nb = s0.shape[0]
    nt = seq // tc
    rb0 = row0 // tc
    rowblk = lambda b, h, t: rb0 + b * nt + t
    kern = functools.partial(_gla_kernel, chunk=chunk, n_chunks=tc // chunk)
    return pl.pallas_call(
        kern,
        grid=(nb, GLA_HEADS, nt),
        in_specs=[
            pl.BlockSpec((tc, GLA_DK), lambda b, h, t: (rowblk(b, h, t), OFF_GLA_Q // GLA_DK + h)),
            pl.BlockSpec((tc, GLA_DK), lambda b, h, t: (rowblk(b, h, t), OFF_GLA_K // GLA_DK + h)),
            pl.BlockSpec((tc, GLA_DV), lambda b, h, t: (rowblk(b, h, t), OFF_GLA_V // GLA_DV + h)),
            pl.BlockSpec((tc, GLA_DV), lambda b, h, t: (rowblk(b, h, t), OFF_GLA_R // GLA_DV + h)),
            pl.BlockSpec((tc, LANES), lambda b, h, t: (rowblk(b, h, t), OFF_SMALL // LANES)),
            pl.BlockSpec((None, LANES, GLA_DK), lambda b, h, t: (h, 0, 0)),
            pl.BlockSpec((1, GLA_DK), lambda b, h, t: (0, h)),
            pl.BlockSpec((1, GLA_DV), lambda b, h, t: (0, 0)),
            pl.BlockSpec((None, None, GLA_DK, GLA_DV), lambda b, h, t: (b, h, 0, 0)),
        ],
        out_specs=[
            pl.BlockSpec((tc, GLA_DV), lambda b, h, t: (b * nt + t, h)),
            pl.BlockSpec((None, None, GLA_DK, GLA_DV), lambda b, h, t: (b, h, 0, 0)),
        ],
        out_shape=[
            jax.ShapeDtypeStruct((nb * seq, MIX_A), BF16),
            jax.ShapeDtypeStruct((nb, GLA_HEADS, GLA_DK, GLA_DV), F32),
        ],
        scratch_shapes=[pltpu.VMEM((GLA_DV, GLA_DK), F32)],
        compiler_params=_cparams("parallel", "parallel", "arbitrary"),
        name="gla_scan",
    )(proj, proj, proj, proj, proj, wg2_pad, b_g.reshape(1, -1), g_out.reshape(1, -1), s0)


def gla_gate_weight(w_g2):
    w = w_g2.reshape(GLA_RANK, GLA_HEADS, GLA_DK).transpose(1, 0, 2)
    return jnp.pad(w, ((0, 0), (SM_GLR, LANES - SM_GLR - GLA_RANK), (0, 0)))


def _dsa_kernel(q_ref, qi_ref, sm_ref, k_ref, v_ref, ki_ref, o_ref,
                key_scr, qis_scr, qg_scr, m_scr, l_scr, acc_scr,
                *, qb, tk, lp, n_sel, l_valid, causal_blocks, qpos_static):
    gq = DSA_GROUP * qb
    if causal_blocks:
        i = pl.program_id(1)
        qpos0 = i * qb
        nt = ((i + 1) * qb + tk - 1) // tk
    else:
        qpos0 = qpos_static
        nt = lp // tk
    nsub = tk // LANES
    idx_bits = max(1, (lp - 1).bit_length())

    for h in range(IDX_HEADS):
        qis_scr[h * qb:(h + 1) * qb, :] = qi_ref[:, h * IDX_DIM:(h + 1) * IDX_DIM]
    for g in range(DSA_KV):
        for hh in range(DSA_GROUP):
            hd = g * DSA_GROUP + hh
            qg_scr[g, hh * qb:(hh + 1) * qb, :] = q_ref[:, hd * HEAD_DIM:(hd + 1) * HEAD_DIM]
    wq = sm_ref[:, SM_IDX_W:SM_IDX_W + IDX_HEADS] * ((IDX_DIM ** -0.5) * (IDX_HEADS ** -0.5))
    w_cols = [wq[:, h:h + 1] for h in range(IDX_HEADS)]
    row_chunk = (qpos0 + lax.broadcasted_iota(I32, (qb, tk), 0)) >> 6
    lane_t = lax.broadcasted_iota(I32, (qb, tk), 1)

    def p1(j, carry):
        ks = pl.multiple_of(j * tk, tk)
        s_all = _dot_nt(qis_scr[...], ki_ref[pl.ds(ks, tk), :])
        acc = jnp.zeros((qb, tk), F32)
        for h in range(IDX_HEADS):
            acc = acc + jnp.maximum(s_all[h * qb:(h + 1) * qb, :], 0.0) * w_cols[h]
        kpos = ks + lane_t
        adm = jnp.where((kpos >> 6) <= row_chunk, kpos, l_valid) < l_valid
        score = jnp.where(adm, acc, -jnp.inf)
        bits = pltpu.bitcast(score, I32)
        key_scr[:, pl.ds(ks, tk)] = bits ^ ((bits >> 31) & 0x7FFFFFFF)
        return carry

    lax.fori_loop(0, nt, p1, 0)

    def count(pred):
        def body(j, acc):
            for c in range(nsub):
                off = pl.multiple_of(j * tk + c * LANES, LANES)
                blk = key_scr[:, pl.ds(off, LANES)]
                idx = off + lax.broadcasted_iota(I32, (qb, LANES), 1)
                acc = acc + jnp.where(pred(blk, idx), 1, 0)
            return acc
        acc = lax.fori_loop(0, nt, body, jnp.zeros((qb, LANES), I32))
        return jnp.sum(acc, axis=1, keepdims=True)

    c0 = count(lambda blk, idx: blk >= 0)
    thr = jnp.where(c0 >= n_sel, 0, INT_MIN).astype(I32)

    def bit_step(it, thr):
        cand = thr + lax.shift_left(jnp.int32(1), 30 - it)
        cnt = count(lambda blk, idx: blk >= cand)
        return jnp.where(cnt >= n_sel, cand, thr)

    thr = lax.fori_loop(0, 31, bit_step, thr)
    cnt_ge = count(lambda blk, idx: blk >= thr)
    tie_rows = jnp.where(thr > KEY_NEG_INF, cnt_ge, 0) > n_sel
    any_tie = jnp.max(jnp.where(tie_rows, 1, 0)) > 0

    def tie_cut():
        need = n_sel - count(lambda blk, idx: blk > thr)
        pos = jnp.zeros((qb, 1), I32)
        for b in range(idx_bits - 1, -1, -1):
            cand = pos + (1 << b)
            f = count(lambda blk, idx: jnp.where(blk == thr, idx, lp) < cand)
            pos = jnp.where(f < need, cand, pos)
        return pos

    cut = lax.cond(any_tie, tie_cut, lambda: jnp.full((qb, 1), lp, I32))

    m_scr[...] = jnp.full(m_scr.shape, M_INIT, F32)
    l_scr[...] = jnp.zeros(l_scr.shape, F32)
    acc_scr[...] = jnp.zeros(acc_scr.shape, F32)

    def p3(j, carry):
        ks = pl.multiple_of(j * tk, tk)
        key = key_scr[:, pl.ds(ks, tk)]
        idx = ks + lane_t
        eq_ok = jnp.where(key == thr, idx, lp + 1) <= cut
        bias = jnp.where(key > thr, 0.0, jnp.where(eq_ok, 0.0, MASK_BIAS))
        bias = jnp.where(key > KEY_NEG_INF, bias, MASK_BIAS)
        bias4 = jnp.concatenate([bias] * DSA_GROUP, axis=0)
        for g in range(DSA_KV):
            kt = k_ref[pl.ds(ks, tk), g * HEAD_DIM:(g + 1) * HEAD_DIM]
            vt = v_ref[pl.ds(ks, tk), g * HEAD_DIM:(g + 1) * HEAD_DIM]
            s = _dot_nt(qg_scr[g], kt) + bias4
            m_old = m_scr[g]
            m_new = jnp.maximum(m_old, jnp.max(s, axis=1, keepdims=True))
            p = jnp.exp(s - m_new)
            alpha = jnp.exp(m_old - m_new)
            l_scr[g] = alpha * l_scr[g] + jnp.sum(p, axis=1, keepdims=True)
            acc_scr[g] = alpha * acc_scr[g] + _dot(p.astype(BF16), vt)
            m_scr[g] = m_new
        return carry

    lax.fori_loop(0, nt, p3, 0)

    for g in range(DSA_KV):
        o = acc_scr[g] / l_scr[g]
        for hh in range(DSA_GROUP):
            hd = g * DSA_GROUP + hh
            o_ref[:, hd * HEAD_DIM:(hd + 1) * HEAD_DIM] = o[hh * qb:(hh + 1) * qb, :].astype(BF16)


def dsa_attention(q, qi, proj, k, v, ki, *, row0, nq, qb, tk, n_sel, l_valid, causal_blocks, qpos_static):
    nb, lp, kvw = k.shape
    rb0 = row0 // qb
    gq = DSA_GROUP * qb
    kern = functools.partial(_dsa_kernel, qb=qb, tk=tk, lp=lp, n_sel=n_sel, l_valid=l_valid,
                             causal_blocks=causal_blocks, qpos_static=qpos_static)
    return pl.pallas_call(
        kern,
        grid=(nb, nq),
        in_specs=[
            pl.BlockSpec((qb, MIX_B), lambda b, i: (rb0 + b * nq + i, 0)),
            pl.BlockSpec((qb, IDX_HEADS * IDX_DIM), lambda b, i: (rb0 + b * nq + i, 0)),
            pl.BlockSpec((qb, LANES), lambda b, i: (rb0 + b * nq + i, OFF_SMALL // LANES)),
            pl.BlockSpec((None, lp, kvw), lambda b, i: (b, 0, 0)),
            pl.BlockSpec((None, lp, kvw), lambda b, i: (b, 0, 0)),
            pl.BlockSpec((None, lp, IDX_DIM), lambda b, i: (b, 0, 0)),
        ],
        out_specs=pl.BlockSpec((qb, MIX_B), lambda b, i: (b * nq + i, 0)),
        out_shape=jax.ShapeDtypeStruct((nb * nq * qb, MIX_B), BF16),
        scratch_shapes=[
            pltpu.VMEM((qb, lp), I32),
            pltpu.VMEM((IDX_HEADS * qb, IDX_DIM), BF16),
            pltpu.VMEM((DSA_KV, gq, HEAD_DIM), BF16),
            pltpu.VMEM((DSA_KV, gq, 1), F32),
            pltpu.VMEM((DSA_KV, gq, 1), F32),
            pltpu.VMEM((DSA_KV, gq, HEAD_DIM), F32),
        ],
        compiler_params=_cparams("parallel", "arbitrary"),
        name="dsa_attention",
    )(q, qi, proj, k, v, ki)


def permute_w_in(w_in):
    offs, o = {}, 0
    for name, size in COL_SIZES:
        offs[name] = (o, size)
        o += size
    parts = [w_in[:, offs[n][0]:offs[n][0] + offs[n][1]] for n in PERM_ORDER]
    used = sum(offs[n][1] for n in PERM_ORDER)
    parts.append(jnp.zeros((w_in.shape[0], P_COLS - used), w_in.dtype))
    return jnp.concatenate(parts, axis=1)


TM_BIG = 1536
TM_DOWN = 768
TC_GLA = 512
TK_DSA = 512
SAMPLE_LP = 2176


def kernel(x_prompt, x_sample, cache_k, cache_v, cache_ki, state_gla, g_attn, w_in, w_gla_gate2, b_gla_gate,
           g_gla_out, g_q, g_k, g_ki, w_branch_a, w_branch_b, w_out, g_ffn, w_ff_gate, w_ff_up, w_ff_down,
           w_router, w_moe_gate, w_moe_up, w_moe_down):
    h = jnp.concatenate([x_prompt.reshape(N_PROMPT, D_MODEL), x_sample.reshape(N_SAMPLE, D_MODEL)], axis=0)
    pos = jnp.concatenate([jnp.tile(jnp.arange(SEQ, dtype=I32), BATCH),
                           jnp.tile(PAST_LEN + jnp.arange(DEC_SEQ, dtype=I32), DEC_BATCH)])
    tabs = rope_tables(pos)
    kvw = DSA_KV * HEAD_DIM
    l_sample = PAST_LEN + DEC_SEQ
    zeros_state = jnp.zeros((BATCH, GLA_HEADS, GLA_DK, GLA_DV), F32)
    outs = {n: [] for n in ('kp', 'vp', 'kip', 'sp', 'ks', 'vs', 'kis', 'ss')}

    for l in range(DEPTH):
        hn = rmsnorm_cast(h, g_attn[l], TM_DOWN)
        proj = matmul(hn, permute_w_in(w_in[l]), TM_BIG, 512, name="in_proj")
        q_b, k_f, k_b, v_b, qi_b, ki_f, ki_b = dsa_prep(proj, tabs, g_q[l], g_k[l], g_ki[l], 512)

        wg2 = gla_gate_weight(w_gla_gate2[l])
        oa_p, s_p = gla_scan(proj, wg2, b_gla_gate[l], g_gla_out[l], zeros_state, 0, SEQ, CHUNK, TC_GLA)
        oa_s, s_s = gla_scan(proj, wg2, b_gla_gate[l], g_gla_out[l], state_gla[l], N_PROMPT, DEC_SEQ,
                             DEC_SEQ, DEC_SEQ)

        ob_p = dsa_attention(
            q_b, qi_b, proj,
            k_b[:N_PROMPT].reshape(BATCH, SEQ, kvw), v_b[:N_PROMPT].reshape(BATCH, SEQ, kvw),
            ki_b[:N_PROMPT].reshape(BATCH, SEQ, IDX_DIM),
            row0=0, nq=SEQ // QBLK, qb=QBLK, tk=TK_DSA, n_sel=min(TOPK_MAX, SEQ // 4), l_valid=SEQ,
            causal_blocks=True, qpos_static=0)

        def with_cache(cache, new, width):
            c = cache.reshape(DEC_BATCH, PAST_LEN, width).astype(BF16)
            n = new[N_PROMPT:].reshape(DEC_BATCH, DEC_SEQ, width)
            pad = jnp.zeros((DEC_BATCH, SAMPLE_LP - l_sample, width), BF16)
            return jnp.concatenate([c, n, pad], axis=1)

        ob_s = dsa_attention(
            q_b, qi_b, proj,
            with_cache(cache_k[l], k_b, kvw), with_cache(cache_v[l], v_b, kvw),
            with_cache(cache_ki[l], ki_b, IDX_DIM),
            row0=N_PROMPT, nq=1, qb=DEC_SEQ, tk=SAMPLE_LP, n_sel=min(TOPK_MAX, l_sample // 4),
            l_valid=l_sample, causal_blocks=False, qpos_static=PAST_LEN)

        o_a = jnp.concatenate([oa_p, oa_s], axis=0)
        o_b = jnp.concatenate([ob_p, ob_s], axis=0)
        merged = merge_branches(o_a, o_b, w_branch_a[l], w_branch_b[l], proj, TM_BIG, 512)
        h = matmul(merged, w_out[l], TM_BIG, 512, residual=h, name="out_proj")

        hn = rmsnorm_cast(h, g_ffn[l], TM_DOWN)
        j = l // 2
        if l % 2 == 0:
            te_big = jnp.zeros((N_TOK // TM_BIG,), I32)
            te_down = jnp.zeros((N_TOK // TM_DOWN,), I32)
            a = swiglu_up(hn, w_ff_gate[j:j + 1], w_ff_up[j:j + 1], te_big, TM_BIG, 256)
            h = swiglu_down(a, w_ff_down[j:j + 1], te_down, h, TM_DOWN, 512)
        else:
            gate = moe_router(hn, w_router[j], TM_DOWN)
            for e in range(N_EXPERTS):
                te_big = jnp.full((N_TOK // TM_BIG,), e, I32)
                te_down = jnp.full((N_TOK // TM_DOWN,), e, I32)
                a = swiglu_up(hn, w_moe_gate[j], w_moe_up[j], te_big, TM_BIG, 256)
                h = swiglu_down(a, w_moe_down[j], te_down, h, TM_DOWN, 512, gate=gate[:, e:e + 1])

        v_f = proj[:, OFF_DSA_V:OFF_DSA_V + kvw]
        outs['kp'].append(k_f[:N_PROMPT].reshape(BATCH, SEQ, DSA_KV, HEAD_DIM))
        outs['vp'].append(v_f[:N_PROMPT].reshape(BATCH, SEQ, DSA_KV, HEAD_DIM))
        outs['kip'].append(ki_f[:N_PROMPT].reshape(BATCH, SEQ, IDX_DIM))
        outs['sp'].append(s_p)
        outs['ks'].append(k_f[N_PROMPT:].reshape(DEC_BATCH, DEC_SEQ, DSA_KV, HEAD_DIM))
        outs['vs'].append(v_f[N_PROMPT:].reshape(DEC_BATCH, DEC_SEQ, DSA_KV, HEAD_DIM))
        outs['kis'].append(ki_f[N_PROMPT:].reshape(DEC_BATCH, DEC_SEQ, IDX_DIM))
        outs['ss'].append(s_s)

    st = {n: jnp.stack(v) for n, v in outs.items()}
    return (h[:N_PROMPT].reshape(BATCH, SEQ, D_MODEL), h[N_PROMPT:].reshape(DEC_BATCH, DEC_SEQ, D_MODEL),
            st['kp'], st['vp'], st['kip'], st['sp'], st['ks'], st['vs'], st['kis'], st['ss'])
```

```python
import functools

import jax
import jax.numpy as jnp
from jax import lax
from jax.experimental import pallas as pl
from jax.experimental.pallas import tpu as pltpu

F32 = jnp.float32
BF16 = jnp.bfloat16
I32 = jnp.int32

D_MODEL = 2048
BATCH = 2
SEQ = 8192
DEPTH = 2
DEC_BATCH = 32
DEC_SEQ = 16
PAST_LEN = 2048
CHUNK = 64
QBLK = 128
ROPE_THETA = 10000.0
EPS = 1e-6
GLA_HEADS = 4
GLA_DK = 128
GLA_DV = 256
GLA_RANK = 16
GLA_TAU = 16.0
DSA_HEADS = 8
DSA_KV = 2
DSA_GROUP = DSA_HEADS // DSA_KV
HEAD_DIM = 128
IDX_HEADS = 8
IDX_DIM = 64
TOPK_MAX = 256
MIX_A = GLA_HEADS * GLA_DV
MIX_B = DSA_HEADS * HEAD_DIM
D_FF = 5632
N_EXPERTS = 8
TOP_K = 2

N_PROMPT = BATCH * SEQ
N_SAMPLE = DEC_BATCH * DEC_SEQ
N_TOK = N_PROMPT + N_SAMPLE

LANES = 128
VMEM_LIMIT_BYTES = 56 * 1024 * 1024

COL_SIZES = (
    ('gla_q', 512), ('gla_k', 512), ('gla_v', 1024), ('gla_glr', 16), ('gla_r', 1024),
    ('dsa_q', 1024), ('dsa_k', 256), ('dsa_v', 256), ('idx_q', 512), ('idx_k', 64),
    ('idx_w', 8), ('gate_a', 2048), ('gate_b', 2048),
)
PERM_ORDER = ('gla_q', 'gla_k', 'gla_v', 'gla_r', 'dsa_q', 'dsa_k', 'dsa_v', 'idx_q',
              'gate_a', 'gate_b', 'idx_k', 'gla_glr', 'idx_w')
P_COLS = 9728
OFF_GLA_Q, OFF_GLA_K, OFF_GLA_V, OFF_GLA_R = 0, 512, 1024, 2048
OFF_DSA_Q, OFF_DSA_K, OFF_DSA_V, OFF_IDX_Q = 3072, 4096, 4352, 4608
OFF_GATE_A, OFF_GATE_B, OFF_SMALL = 5120, 7168, 9216
SM_IDX_K, SM_GLR, SM_IDX_W = 0, 64, 80

ROUTE_E1, ROUTE_E2, ROUTE_W1, ROUTE_W2 = 0, 1, 2, 3
HI16 = -65536
KEY_NEG_INF = -2139095041
INT_MIN = -2147483648
MASK_BIAS = -1e30
M_INIT = -5e29
Q_SCALE = (HEAD_DIM ** -0.5) * 1.4426950408889634


def _cparams(*sem):
    return pltpu.CompilerParams(dimension_semantics=sem, vmem_limit_bytes=VMEM_LIMIT_BYTES)


def _dot(a, b):
    return jnp.dot(a, b, preferred_element_type=F32)


def _dot_nt(a, b):
    return lax.dot_general(a, b, (((1,), (1,)), ((), ())), preferred_element_type=F32)


def _dot_tn(a, b):
    return lax.dot_general(a, b, (((0,), (0,)), ((), ())), preferred_element_type=F32)


def _split3(a):
    a1 = a.astype(BF16)
    r1 = a - a1.astype(F32)
    a2 = r1.astype(BF16)
    a3 = (r1 - a2.astype(F32)).astype(BF16)
    return a1, a2, a3


def _rmsnorm_kernel(x_ref, g_ref, o_ref):
    x = x_ref[...]
    ms = jnp.mean(x * x, axis=-1, keepdims=True)
    o_ref[...] = ((x * lax.rsqrt(ms + EPS)) * g_ref[...]).astype(BF16)


def rmsnorm_cast(x, g, tm):
    n, d = x.shape
    return pl.pallas_call(
        _rmsnorm_kernel,
        grid=(n // tm,),
        in_specs=[pl.BlockSpec((tm, d), lambda i: (i, 0)), pl.BlockSpec((1, d), lambda i: (0, 0))],
        out_specs=pl.BlockSpec((tm, d), lambda i: (i, 0)),
        out_shape=jax.ShapeDtypeStruct((n, d), BF16),
        compiler_params=_cparams("parallel"),
        name="rmsnorm_cast",
    )(x, g.reshape(1, d))


def _mm_kernel(x_ref, w_ref, o_ref):
    o_ref[...] = _dot(x_ref[...], w_ref[...].astype(BF16))


def _mm_res_kernel(x_ref, w_ref, r_ref, o_ref):
    o_ref[...] = r_ref[...] + _dot(x_ref[...], w_ref[...].astype(BF16))


def matmul(x, w, tm, tn, residual=None, name="matmul", weight_resident=False):
    m, k = x.shape
    n = w.shape[1]
    if weight_resident:
        grid = (n // tn, m // tm)
        rc = lambda a, b: (b, a)
    else:
        grid = (m // tm, n // tn)
        rc = lambda a, b: (a, b)
    in_specs = [pl.BlockSpec((tm, k), lambda a, b: (rc(a, b)[0], 0)),
                pl.BlockSpec((k, tn), lambda a, b: (0, rc(a, b)[1]))]
    args = [x, w]
    body = _mm_kernel
    if residual is not None:
        in_specs.append(pl.BlockSpec((tm, tn), lambda a, b: rc(a, b)))
        args.append(residual)
        body = _mm_res_kernel
    return pl.pallas_call(
        body,
        grid=grid,
        in_specs=in_specs,
        out_specs=pl.BlockSpec((tm, tn), lambda a, b: rc(a, b)),
        out_shape=jax.ShapeDtypeStruct((m, n), F32),
        compiler_params=_cparams("parallel", "parallel"),
        name=name,
    )(*args)


def _merge_kernel(oa_ref, ob_ref, wa_ref, wb_ref, ga_ref, gb_ref, o_ref):
    a = _dot(oa_ref[...], wa_ref[...].astype(BF16))
    b = _dot(ob_ref[...], wb_ref[...].astype(BF16))
    o_ref[...] = (jax.nn.sigmoid(ga_ref[...]) * a + jax.nn.sigmoid(gb_ref[...]) * b).astype(BF16)


def merge_branches(o_a, o_b, w_pa, w_pb, proj, tm, tn):
    m = o_a.shape[0]
    n = w_pa.shape[1]
    ja, jb = OFF_GATE_A // tn, OFF_GATE_B // tn
    return pl.pallas_call(
        _merge_kernel,
        grid=(m // tm, n // tn),
        in_specs=[
            pl.BlockSpec((tm, MIX_A), lambda i, j: (i, 0)),
            pl.BlockSpec((tm, MIX_B), lambda i, j: (i, 0)),
            pl.BlockSpec((MIX_A, tn), lambda i, j: (0, j)),
            pl.BlockSpec((MIX_B, tn), lambda i, j: (0, j)),
            pl.BlockSpec((tm, tn), lambda i, j: (i, ja + j)),
            pl.BlockSpec((tm, tn), lambda i, j: (i, jb + j)),
        ],
        out_specs=pl.BlockSpec((tm, tn), lambda i, j: (i, j)),
        out_shape=jax.ShapeDtypeStruct((m, n), BF16),
        compiler_params=_cparams("parallel", "parallel"),
        name="merge_branches",
    )(o_a, o_b, w_pa, w_pb, proj, proj)


def _swiglu_kernel(x_ref, wg_ref, wu_ref, o_ref):
    x = x_ref[...]
    a = _dot(x, wg_ref[...].astype(BF16))
    b = _dot(x, wu_ref[...].astype(BF16))
    o_ref[...] = (a * jax.nn.sigmoid(a) * b).astype(BF16)


def swiglu_up(x, w_gate, w_up, tm, tn):
    m, k = x.shape
    f = w_gate.shape[1]
    return pl.pallas_call(
        _swiglu_kernel,
        grid=(m // tm, f // tn),
        in_specs=[
            pl.BlockSpec((tm, k), lambda i, j: (i, 0)),
            pl.BlockSpec((k, tn), lambda i, j: (0, j)),
            pl.BlockSpec((k, tn), lambda i, j: (0, j)),
        ],
        out_specs=pl.BlockSpec((tm, tn), lambda i, j: (i, j)),
        out_shape=jax.ShapeDtypeStruct((m, f), BF16),
        compiler_params=_cparams("parallel", "parallel"),
        name="swiglu_up",
    )(x, w_gate, w_up)


def _router_kernel(x_ref, w_ref, o_ref):
    logits = _dot(x_ref[...], w_ref[...].astype(BF16))
    lane = lax.broadcasted_iota(I32, logits.shape, 1)
    logits = jnp.where(lane < N_EXPERTS, logits, -jnp.inf)
    t1 = jnp.max(logits, axis=-1, keepdims=True)
    i1 = jnp.min(jnp.where(logits == t1, lane, LANES), axis=-1, keepdims=True)
    rest = jnp.where(lane == i1, -jnp.inf, logits)
    t2 = jnp.max(rest, axis=-1, keepdims=True)
    i2 = jnp.min(jnp.where(rest == t2, lane, LANES), axis=-1, keepdims=True)
    e2 = jnp.exp(t2 - t1)
    den = 1.0 + e2
    o_ref[...] = jnp.where(lane == ROUTE_E1, i1.astype(F32), 0.0) + jnp.where(lane == ROUTE_E2, i2.astype(F32), 0.0) \
        + jnp.where(lane == ROUTE_W1, 1.0 / den, 0.0) + jnp.where(lane == ROUTE_W2, e2 / den, 0.0)


def moe_router(x, w_router, tm):
    m, k = x.shape
    w_pad = jnp.pad(w_router, ((0, 0), (0, LANES - w_router.shape[1])))
    return pl.pallas_call(
        _router_kernel,
        grid=(m // tm,),
        in_specs=[pl.BlockSpec((tm, k), lambda i: (i, 0)), pl.BlockSpec((k, LANES), lambda i: (0, 0))],
        out_specs=pl.BlockSpec((tm, LANES), lambda i: (i, 0)),
        out_shape=jax.ShapeDtypeStruct((m, LANES), F32),
        compiler_params=_cparams("parallel"),
        name="moe_router",
    )(x, w_pad)


def _rmsnorm_pack_kernel(x_ref, g_ref, o_ref, p_ref):
    x = x_ref[...]
    ms = jnp.mean(x * x, axis=-1, keepdims=True)
    y = ((x * lax.rsqrt(ms + EPS)) * g_ref[...]).astype(BF16)
    o_ref[...] = y
    bits = pltpu.bitcast(y.astype(F32), I32)
    half = x.shape[1] // 2
    p_ref[...] = (bits[:, half:] & HI16) | lax.shift_right_logical(bits[:, :half], 16)


def rmsnorm_pack(x, g, tm):
    n, d = x.shape
    return pl.pallas_call(
        _rmsnorm_pack_kernel,
        grid=(n // tm,),
        in_specs=[pl.BlockSpec((tm, d), lambda i: (i, 0)), pl.BlockSpec((1, d), lambda i: (0, 0))],
        out_specs=[pl.BlockSpec((tm, d), lambda i: (i, 0)), pl.BlockSpec((tm, d // 2), lambda i: (i, 0))],
        out_shape=[jax.ShapeDtypeStruct((n, d), BF16), jax.ShapeDtypeStruct((n, d // 2), I32)],
        compiler_params=_cparams("parallel"),
        name="rmsnorm_pack",
    )(x, g.reshape(1, d))


def _unpack_bf16(p):
    lo = pltpu.bitcast(lax.shift_left(p, 16), F32).astype(BF16)
    hi = pltpu.bitcast(p & HI16, F32).astype(BF16)
    return lo, hi


def moe_plan(route, tm, m_pad):
    n = route.shape[0]
    ef = route[:, ROUTE_E1:ROUTE_E2 + 1].astype(I32).reshape(2 * n)
    onehot = (ef[:, None] == jnp.arange(N_EXPERTS, dtype=I32)[None, :]).astype(I32)
    counts = jnp.sum(onehot, axis=0)
    rank = jnp.sum((jnp.cumsum(onehot, axis=0) - onehot) * onehot, axis=1)
    padded = ((counts + tm - 1) // tm) * tm
    ends = jnp.cumsum(padded)
    dest = jnp.sum(onehot * (ends - padded)[None, :], axis=1) + rank
    tile_start = jnp.arange(m_pad // tm, dtype=I32) * tm
    tile_expert = jnp.minimum(jnp.sum((tile_start[:, None] >= ends[None, :]).astype(I32), axis=1), N_EXPERTS - 1)
    n_valid = (ends[-1] // tm).reshape(1)
    return dest, tile_expert, n_valid


def _dispatch_kernel(dest_ref, x_ref, xs_in_ref, xs_ref, sem):
    del xs_in_ref
    rows = x_ref.shape[0]

    def row_copy(r, s):
        d = dest_ref[0, 0, TOP_K * r + s]
        return pltpu.make_async_copy(x_ref.at[pl.ds(r, 1), :], xs_ref.at[pl.ds(d, 1), :], sem)

    def start(r, c):
        for s in range(TOP_K):
            row_copy(r, s).start()
        return c

    def wait(r, c):
        for s in range(TOP_K):
            row_copy(r, s).wait()
        return c

    lax.fori_loop(0, rows, start, 0)
    lax.fori_loop(0, rows, wait, 0)


def moe_dispatch(xp, dest, m_pad, tr):
    n, w = xp.shape
    return pl.pallas_call(
        _dispatch_kernel,
        grid=(n // tr,),
        in_specs=[
            pl.BlockSpec((1, 1, TOP_K * tr), lambda i: (i, 0, 0), memory_space=pltpu.SMEM),
            pl.BlockSpec((tr, w), lambda i: (i, 0)),
            pl.BlockSpec(memory_space=pl.ANY),
        ],
        out_specs=pl.BlockSpec(memory_space=pl.ANY),
        out_shape=jax.ShapeDtypeStruct((m_pad, w), I32),
        scratch_shapes=[pltpu.SemaphoreType.DMA(())],
        input_output_aliases={2: 0},
        compiler_params=_cparams("arbitrary"),
        name="moe_dispatch",
    )(dest.reshape(n // tr, 1, TOP_K * tr), xp, jnp.zeros((m_pad, w), I32))


def _moe_up_kernel(te_ref, nv_ref, x_ref, wg_ref, wu_ref, o_ref):
    del te_ref
    valid = pl.program_id(1) < nv_ref[0]

    @pl.when(valid)
    def _():
        lo, hi = _unpack_bf16(x_ref[...])
        half = x_ref.shape[1]
        a = _dot(lo, wg_ref[:half, :].astype(BF16)) + _dot(hi, wg_ref[half:, :].astype(BF16))
        b = _dot(lo, wu_ref[:half, :].astype(BF16)) + _dot(hi, wu_ref[half:, :].astype(BF16))
        o_ref[...] = (a * jax.nn.sigmoid(a) * b).astype(BF16)

    @pl.when(jnp.logical_not(valid))
    def _():
        o_ref[...] = jnp.zeros_like(o_ref)


def moe_up(xs, w_gate, w_up, tile_expert, n_valid, tm, tn):
    m = xs.shape[0]
    _, k, f = w_gate.shape
    grid_spec = pltpu.PrefetchScalarGridSpec(
        num_scalar_prefetch=2,
        grid=(f // tn, m // tm),
        in_specs=[
            pl.BlockSpec((tm, k // 2), lambda j, i, te, nv: (i, 0)),
            pl.BlockSpec((None, k, tn), lambda j, i, te, nv: (te[i], 0, j)),
            pl.BlockSpec((None, k, tn), lambda j, i, te, nv: (te[i], 0, j)),
        ],
        out_specs=pl.BlockSpec((tm, tn), lambda j, i, te, nv: (i, j)),
    )
    return pl.pallas_call(
        _moe_up_kernel,
        grid_spec=grid_spec,
        out_shape=jax.ShapeDtypeStruct((m, f), BF16),
        compiler_params=_cparams("parallel", "arbitrary"),
        name="moe_up",
    )(tile_expert, n_valid, xs, w_gate, w_up)


def _moe_down_kernel(te_ref, nv_ref, a_ref, w_ref, o_ref):
    del te_ref
    valid = pl.program_id(1) < nv_ref[0]

    @pl.when(valid)
    def _():
        o_ref[...] = _dot(a_ref[...], w_ref[...].astype(BF16))

    @pl.when(jnp.logical_not(valid))
    def _():
        o_ref[...] = jnp.zeros_like(o_ref)


def moe_down(a, w_down, tile_expert, n_valid, tm, tn):
    m, f = a.shape
    d = w_down.shape[2]
    grid_spec = pltpu.PrefetchScalarGridSpec(
        num_scalar_prefetch=2,
        grid=(d // tn, m // tm),
        in_specs=[
            pl.BlockSpec((tm, f), lambda j, i, te, nv: (i, 0)),
            pl.BlockSpec((None, f, tn), lambda j, i, te, nv: (te[i], 0, j)),
        ],
        out_specs=pl.BlockSpec((tm, tn), lambda j, i, te, nv: (i, j)),
    )
    return pl.pallas_call(
        _moe_down_kernel,
        grid_spec=grid_spec,
        out_shape=jax.ShapeDtypeStruct((m, d), F32),
        compiler_params=_cparams("parallel", "arbitrary"),
        name="moe_down",
    )(tile_expert, n_valid, a, w_down)


def _combine_kernel(dest_ref, y_ref, h_ref, rt_ref, o_ref, buf_ref, sem):
    rows = h_ref.shape[0]

    def row_copy(r, s):
        d = dest_ref[0, 0, TOP_K * r + s]
        return pltpu.make_async_copy(y_ref.at[pl.ds(d, 1), :], buf_ref.at[s, pl.ds(r, 1), :], sem.at[s])

    def start(r, c):
        for s in range(TOP_K):
            row_copy(r, s).start()
        return c

    def wait(r, c):
        for s in range(TOP_K):
            row_copy(r, s).wait()
        return c

    lax.fori_loop(0, rows, start, 0)
    lax.fori_loop(0, rows, wait, 0)
    rt = rt_ref[...]
    o_ref[...] = h_ref[...] + (rt[:, ROUTE_W1:ROUTE_W1 + 1] * buf_ref[0] + rt[:, ROUTE_W2:ROUTE_W2 + 1] * buf_ref[1])


def moe_combine(y, dest, route, h, tr):
    n, d = h.shape
    return pl.pallas_call(
        _combine_kernel,
        grid=(n // tr,),
        in_specs=[
            pl.BlockSpec((1, 1, TOP_K * tr), lambda i: (i, 0, 0), memory_space=pltpu.SMEM),
            pl.BlockSpec(memory_space=pl.ANY),
            pl.BlockSpec((tr, d), lambda i: (i, 0)),
            pl.BlockSpec((tr, LANES), lambda i: (i, 0)),
        ],
        out_specs=pl.BlockSpec((tr, d), lambda i: (i, 0)),
        out_shape=jax.ShapeDtypeStruct((n, d), F32),
        scratch_shapes=[pltpu.VMEM((TOP_K, tr, d), F32), pltpu.SemaphoreType.DMA((TOP_K,))],
        compiler_params=_cparams("arbitrary"),
        name="moe_combine",
    )(dest.reshape(n // tr, 1, TOP_K * tr), y, h, route)


def _prep_kernel(q_ref, k_ref, v_ref, qi_ref, sm_ref, c128_ref, s128_ref, c64_ref, s64_ref,
                 gq_ref, gk_ref, gki_ref,
                 qo_ref, kf_ref, kb_ref, vb_ref, qio_ref, kif_ref, kib_ref):
    c128, s128 = c128_ref[...], s128_ref[...]
    c64, s64 = c64_ref[...], s64_ref[...]
    lane = lax.broadcasted_iota(I32, c64.shape, 1)
    lane_lo = (lane & 63) < 32

    def rope128(x):
        return x * c128 + pltpu.roll(x, 64, 1) * s128

    def rope64(x):
        partner = jnp.where(lane_lo, pltpu.roll(x, 96, 1), pltpu.roll(x, 32, 1))
        return x * c64 + partner * s64

    def norm128(x, g):
        ms = jnp.mean(x * x, axis=-1, keepdims=True)
        return x * lax.rsqrt(ms + EPS) * g

    gq, gk = gq_ref[...], gk_ref[...]
    for h in range(DSA_HEADS):
        sl = slice(h * HEAD_DIM, (h + 1) * HEAD_DIM)
        qo_ref[:, sl] = (rope128(norm128(q_ref[:, sl], gq)) * Q_SCALE).astype(BF16)
    for h in range(DSA_KV):
        sl = slice(h * HEAD_DIM, (h + 1) * HEAD_DIM)
        kr = rope128(norm128(k_ref[:, sl], gk))
        kf_ref[:, sl] = kr
        kb_ref[:, sl] = kr.astype(BF16)
    ones = jnp.ones((v_ref.shape[0], HEAD_DIM), BF16)
    for h in range(DSA_KV):
        vb_ref[:, 2 * h * HEAD_DIM:(2 * h + 1) * HEAD_DIM] = v_ref[:, h * HEAD_DIM:(h + 1) * HEAD_DIM].astype(BF16)
        vb_ref[:, (2 * h + 1) * HEAD_DIM:(2 * h + 2) * HEAD_DIM] = ones
    for t in range(IDX_HEADS * IDX_DIM // LANES):
        sl = slice(t * LANES, (t + 1) * LANES)
        qio_ref[:, sl] = rope64(qi_ref[:, sl]).astype(BF16)
    sm = sm_ref[...]
    ms = jnp.sum(jnp.where(lane < IDX_DIM, sm * sm, 0.0), axis=-1, keepdims=True) * (1.0 / IDX_DIM)
    ki = rope64(sm * lax.rsqrt(ms + EPS) * gki_ref[...])
    kif_ref[...] = ki[:, :IDX_DIM]
    kib_ref[...] = ki[:, :IDX_DIM].astype(BF16)


def dsa_prep(proj, tabs, g_q, g_k, g_ki, tm):
    n = proj.shape[0]
    c128, s128, c64, s64 = tabs
    gki_pad = jnp.pad(g_ki, (0, LANES - IDX_DIM)).reshape(1, LANES)
    row = lambda w, off: pl.BlockSpec((tm, w), lambda i: (i, off // w))
    tab = pl.BlockSpec((tm, LANES), lambda i: (i, 0))
    vec = pl.BlockSpec((1, LANES), lambda i: (0, 0))
    out = lambda w: pl.BlockSpec((tm, w), lambda i: (i, 0))
    kvw = DSA_KV * HEAD_DIM
    return pl.pallas_call(
        _prep_kernel,
        grid=(n // tm,),
        in_specs=[row(MIX_B, OFF_DSA_Q), row(kvw, OFF_DSA_K), row(kvw, OFF_DSA_V),
                  row(IDX_HEADS * IDX_DIM, OFF_IDX_Q), row(LANES, OFF_SMALL),
                  tab, tab, tab, tab, vec, vec, vec],
        out_specs=[out(MIX_B), out(kvw), out(kvw), out(2 * kvw), out(IDX_HEADS * IDX_DIM),
                   out(IDX_DIM), out(IDX_DIM)],
        out_shape=[
            jax.ShapeDtypeStruct((n, MIX_B), BF16),
            jax.ShapeDtypeStruct((n, kvw), F32),
            jax.ShapeDtypeStruct((n, kvw), BF16),
            jax.ShapeDtypeStruct((n, 2 * kvw), BF16),
            jax.ShapeDtypeStruct((n, IDX_HEADS * IDX_DIM), BF16),
            jax.ShapeDtypeStruct((n, IDX_DIM), F32),
            jax.ShapeDtypeStruct((n, IDX_DIM), BF16),
        ],
        compiler_params=_cparams("parallel"),
        name="dsa_prep",
    )(proj, proj, proj, proj, proj, c128, s128, c64, s64,
      g_q.reshape(1, LANES), g_k.reshape(1, LANES), gki_pad)


def rope_tables(pos):
    def tab(half, reps):
        inv = ROPE_THETA ** (-jnp.arange(half, dtype=F32) / half)
        ang = pos.astype(F32)[:, None] * inv[None, :]
        cos, sin = jnp.cos(ang), jnp.sin(ang)
        return jnp.tile(jnp.concatenate([cos, cos], -1), (1, reps)), jnp.tile(jnp.concatenate([-sin, sin], -1), (1, reps))
    c128, s128 = tab(HEAD_DIM // 2, 1)
    c64, s64 = tab(IDX_DIM // 2, 2)
    return c128, s128, c64, s64


def _gla_kernel(q_ref, k_ref, v_ref, r_ref, sm_ref, wg2_ref, bg_ref, gout_ref, s0_ref,
                o_ref, sout_ref, st_ref, *, chunk, n_chunks):
    t = pl.program_id(2)

    @pl.when(t == 0)
    def _():
        st_ref[...] = s0_ref[...].T

    row = lax.broadcasted_iota(I32, (chunk, chunk), 0)
    col = lax.broadcasted_iota(I32, (chunk, chunk), 1)
    causal = row >= col
    tril = jnp.where(causal, 1.0, 0.0).astype(BF16)
    w1, w2, w3 = _split3(wg2_ref[...])
    bg = bg_ref[...]
    gout = gout_ref[...]

    for c in range(n_chunks):
        sl = pl.ds(c * chunk, chunk)
        q = q_ref[sl, :] * (GLA_DK ** -0.5)
        k = k_ref[sl, :]
        v = v_ref[sl, :].astype(BF16)
        s1, s2, s3 = _split3(sm_ref[sl, :])
        z = (_dot(s1, w1) + (_dot(s1, w2) + _dot(s2, w1)) + (_dot(s1, w3) + _dot(s2, w2) + _dot(s3, w1))) + bg
        g = (jnp.minimum(z, 0.0) - jnp.log1p(jnp.exp(-jnp.abs(z)))) * (1.0 / GLA_TAU)
        g1, g2, g3 = _split3(g)
        b = _dot(tril, g1) + _dot(tril, g2) + _dot(tril, g3)
        b_last = b[chunk - 1:chunk, :]
        qe = (q * jnp.exp(b)).astype(BF16)
        ke = (k * jnp.exp(-b)).astype(BF16)
        kd = (k * jnp.exp(b_last - b)).astype(BF16)
        a = jnp.where(causal, _dot_nt(qe, ke), 0.0).astype(BF16)
        st = st_ref[...]
        o = _dot(a, v) + _dot_nt(qe, st.astype(BF16))
        st_ref[...] = jnp.exp(b_last) * st + _dot_tn(v, kd)
        ms = jnp.mean(o * o, axis=-1, keepdims=True)
        on = o * lax.rsqrt(ms + EPS) * gout
        r = r_ref[sl, :]
        o_ref[sl, :] = (r * jax.nn.sigmoid(r) * on).astype(BF16)

    @pl.when(t == pl.num_programs(2) - 1)
    def _():
        sout_ref[...] = st_ref[...].T


def gla_scan(proj, wg2_pad, b_g, g_out, s0, row0, seq, chunk, tc):
    nb = s0.shape[0]
    nt = seq // tc
    rb0 = row0 // tc
    rowblk = lambda b, h, t: rb0 + b * nt + t
    kern = functools.partial(_gla_kernel, chunk=chunk, n_chunks=tc // chunk)
    return pl.pallas_call(
        kern,
        grid=(nb, GLA_HEADS, nt),
        in_specs=[
            pl.BlockSpec((tc, GLA_DK), lambda b, h, t: (rowblk(b, h, t), OFF_GLA_Q // GLA_DK + h)),
            pl.BlockSpec((tc, GLA_DK), lambda b, h, t: (rowblk(b, h, t), OFF_GLA_K // GLA_DK + h)),
            pl.BlockSpec((tc, GLA_DV), lambda b, h, t: (rowblk(b, h, t), OFF_GLA_V // GLA_DV + h)),
            pl.BlockSpec((tc, GLA_DV), lambda b, h, t: (rowblk(b, h, t), OFF_GLA_R // GLA_DV + h)),
            pl.BlockSpec((tc, LANES), lambda b, h, t: (rowblk(b, h, t), OFF_SMALL // LANES)),
            pl.BlockSpec((None, LANES, GLA_DK), lambda b, h, t: (h, 0, 0)),
            pl.BlockSpec((1, GLA_DK), lambda b, h, t: (0, h)),
            pl.BlockSpec((1, GLA_DV), lambda b, h, t: (0, 0)),
            pl.BlockSpec((None, None, GLA_DK, GLA_DV), lambda b, h, t: (b, h, 0, 0)),
        ],
        out_specs=[
            pl.BlockSpec((tc, GLA_DV), lambda b, h, t: (b * nt + t, h)),
            pl.BlockSpec((None, None, GLA_DK, GLA_DV), lambda b, h, t: (b, h, 0, 0)),
        ],
        out_shape=[
            jax.ShapeDtypeStruct((nb * seq, MIX_A), BF16),
            jax.ShapeDtypeStruct((nb, GLA_HEADS, GLA_DK, GLA_DV), F32),
        ],
        scratch_shapes=[pltpu.VMEM((GLA_DV, GLA_DK), F32)],
        compiler_params=_cparams("parallel", "parallel", "arbitrary"),
        name="gla_scan",
    )(proj, proj, proj, proj, proj, wg2_pad, b_g.reshape(1, -1), g_out.reshape(1, -1), s0)


def gla_gate_weight(w_g2):
    w = w_g2.reshape(GLA_RANK, GLA_HEADS, GLA_DK).transpose(1, 0, 2)
    return jnp.pad(w, ((0, 0), (SM_GLR, LANES - SM_GLR - GLA_RANK), (0, 0)))


def _dsa_kernel(q_ref, qi_ref, sm_ref, k_ref, v_ref, ki_ref, o_ref,
                key_scr, qis_scr, qg_scr, m_scr, a_scr, p_scr, acc_scr,
                *, qb, tk, lp, n_sel, l_valid, causal_blocks, qpos_static):
    gq = DSA_GROUP * qb
    if causal_blocks:
        i = pl.program_id(1)
        qpos0 = i * qb
        nt = ((i + 1) * qb + tk - 1) // tk
    else:
        qpos0 = qpos_static
        nt = lp // tk
    nsub = tk // LANES
    idx_bits = max(1, (lp - 1).bit_length())

    for h in range(IDX_HEADS):
        qis_scr[h * qb:(h + 1) * qb, :] = qi_ref[:, h * IDX_DIM:(h + 1) * IDX_DIM]
    for g in range(DSA_KV):
        for hh in range(DSA_GROUP):
            hd = g * DSA_GROUP + hh
            qg_scr[g, hh * qb:(hh + 1) * qb, :] = q_ref[:, hd * HEAD_DIM:(hd + 1) * HEAD_DIM]
    wq = sm_ref[:, SM_IDX_W:SM_IDX_W + IDX_HEADS] * ((IDX_DIM ** -0.5) * (IDX_HEADS ** -0.5))
    w_cols = [wq[:, h:h + 1] for h in range(IDX_HEADS)]
    row_chunk = (qpos0 + lax.broadcasted_iota(I32, (qb, tk), 0)) >> 6
    lane_t = lax.broadcasted_iota(I32, (qb, tk), 1)

    def p1(j, carry):
        ks = pl.multiple_of(j * tk, tk)
        s_all = _dot_nt(qis_scr[...], ki_ref[pl.ds(ks, tk), :])
        acc = jnp.zeros((qb, tk), F32)
        for h in range(IDX_HEADS):
            acc = acc + jnp.maximum(s_all[h * qb:(h + 1) * qb, :], 0.0) * w_cols[h]
        kpos = ks + lane_t
        adm = jnp.where((kpos >> 6) <= row_chunk, kpos, l_valid) < l_valid
        score = jnp.where(adm, acc, -jnp.inf)
        bits = pltpu.bitcast(score, I32)
        key_scr[:, pl.ds(ks, tk)] = bits ^ ((bits >> 31) & 0x7FFFFFFF)
        return carry

    lax.fori_loop(0, nt, p1, 0)

    def count(pred):
        def body(j, acc):
            for c in range(nsub):
                off = pl.multiple_of(j * tk + c * LANES, LANES)
                blk = key_scr[:, pl.ds(off, LANES)]
                idx = off + lax.broadcasted_iota(I32, (qb, LANES), 1)
                acc = acc + jnp.where(pred(blk, idx), 1, 0)
            return acc
        acc = lax.fori_loop(0, nt, body, jnp.zeros((qb, LANES), I32))
        return jnp.sum(acc, axis=1, keepdims=True)

    c0 = count(lambda blk, idx: blk >= 0)
    thr = jnp.where(c0 >= n_sel, 0, INT_MIN).astype(I32)

    def bit_step(it, thr):
        cand = thr + lax.shift_left(jnp.int32(1), 30 - it)
        cnt = count(lambda blk, idx: blk >= cand)
        return jnp.where(cnt >= n_sel, cand, thr)

    thr = lax.fori_loop(0, 31, bit_step, thr)
    cnt_ge = count(lambda blk, idx: blk >= thr)
    tie_rows = jnp.where(thr > KEY_NEG_INF, cnt_ge, 0) > n_sel
    any_tie = jnp.max(jnp.where(tie_rows, 1, 0)) > 0

    def tie_cut():
        need = n_sel - count(lambda blk, idx: blk > thr)
        pos = jnp.zeros((qb, 1), I32)
        for b in range(idx_bits - 1, -1, -1):
            cand = pos + (1 << b)
            f = count(lambda blk, idx: jnp.where(blk == thr, idx, lp) < cand)
            pos = jnp.where(f < need, cand, pos)
        return pos

    cut = lax.cond(any_tie, tie_cut, lambda: jnp.full((qb, 1), lp, I32))

    m_scr[...] = jnp.full(m_scr.shape, M_INIT, F32)
    acc_scr[...] = jnp.zeros(acc_scr.shape, F32)
    rb = min(qb, 32)

    def p3(j, carry):
        ks = pl.multiple_of(j * tk, tk)
        key = key_scr[:, pl.ds(ks, tk)]
        idx = ks + lane_t
        eq_ok = jnp.where(key == thr, idx, lp + 1) <= cut
        bias = jnp.where(key > thr, 0.0, jnp.where(eq_ok, 0.0, MASK_BIAS))
        bias = jnp.where(key > KEY_NEG_INF, bias, MASK_BIAS)
        for g in range(DSA_KV):
            kt = k_ref[pl.ds(ks, tk), g * HEAD_DIM:(g + 1) * HEAD_DIM]
            vt = v_ref[pl.ds(ks, tk), 2 * g * HEAD_DIM:(2 * g + 2) * HEAD_DIM]
            s_all = _dot_nt(qg_scr[g], kt)
            for r0 in range(0, gq, rb):
                rs = slice(r0, r0 + rb)
                s = s_all[rs, :] + bias[r0 % qb:r0 % qb + rb, :]
                m_old = m_scr[g, rs, :]
                m_new = jnp.maximum(m_old, jnp.max(s, axis=1, keepdims=True))
                p_scr[rs, :] = jnp.exp2(s - m_new).astype(BF16)
                a_scr[rs, :] = jnp.exp2(m_old - m_new)
                m_scr[g, rs, :] = m_new
            acc_scr[g] = a_scr[...] * acc_scr[g] + _dot(p_scr[...], vt)
        return carry

    lax.fori_loop(0, nt, p3, 0)

    for g in range(DSA_KV):
        acc = acc_scr[g]
        o = acc[:, :HEAD_DIM] / acc[:, HEAD_DIM:]
        for hh in range(DSA_GROUP):
            hd = g * DSA_GROUP + hh
            o_ref[:, hd * HEAD_DIM:(hd + 1) * HEAD_DIM] = o[hh * qb:(hh + 1) * qb, :].astype(BF16)


def dsa_attention(q, qi, proj, k, v, ki, *, row0, nq, qb, tk, n_sel, l_valid, causal_blocks, qpos_static):
    nb, lp, kvw = k.shape
    rb0 = row0 // qb
    gq = DSA_GROUP * qb
    kern = functools.partial(_dsa_kernel, qb=qb, tk=tk, lp=lp, n_sel=n_sel, l_valid=l_valid,
                             causal_blocks=causal_blocks, qpos_static=qpos_static)
    return pl.pallas_call(
        kern,
        grid=(nb, nq),
        in_specs=[
            pl.BlockSpec((qb, MIX_B), lambda b, i: (rb0 + b * nq + i, 0)),
            pl.BlockSpec((qb, IDX_HEADS * IDX_DIM), lambda b, i: (rb0 + b * nq + i, 0)),
            pl.BlockSpec((qb, LANES), lambda b, i: (rb0 + b * nq + i, OFF_SMALL // LANES)),
            pl.BlockSpec((None, lp, kvw), lambda b, i: (b, 0, 0)),
            pl.BlockSpec((None, lp, 2 * kvw), lambda b, i: (b, 0, 0)),
            pl.BlockSpec((None, lp, IDX_DIM), lambda b, i: (b, 0, 0)),
        ],
        out_specs=pl.BlockSpec((qb, MIX_B), lambda b, i: (b * nq + i, 0)),
        out_shape=jax.ShapeDtypeStruct((nb * nq * qb, MIX_B), BF16),
        scratch_shapes=[
            pltpu.VMEM((qb, lp), I32),
            pltpu.VMEM((IDX_HEADS * qb, IDX_DIM), BF16),
            pltpu.VMEM((DSA_KV, gq, HEAD_DIM), BF16),
            pltpu.VMEM((DSA_KV, gq, 1), F32),
            pltpu.VMEM((gq, 1), F32),
            pltpu.VMEM((gq, tk), BF16),
            pltpu.VMEM((DSA_KV, gq, 2 * HEAD_DIM), F32),
        ],
        compiler_params=_cparams("parallel", "arbitrary"),
        name="dsa_attention",
    )(q, qi, proj, k, v, ki)


def permute_w_in(w_in):
    offs, o = {}, 0
    for name, size in COL_SIZES:
        offs[name] = (o, size)
        o += size
    parts = [w_in[:, offs[n][0]:offs[n][0] + offs[n][1]] for n in PERM_ORDER]
    used = sum(offs[n][1] for n in PERM_ORDER)
    parts.append(jnp.zeros((w_in.shape[0], P_COLS - used), w_in.dtype))
    return jnp.concatenate(parts, axis=1)


TM_BIG = 1536
TM_DOWN = 768
TC_GLA = 512
TK_DSA = 512
SAMPLE_LP = 2176
TM_MOE = 512
TR_MOE = 256
MOE_ROWS = -(-(TOP_K * N_TOK + N_EXPERTS * (TM_MOE - 1)) // TM_MOE) * TM_MOE


def kernel(x_prompt, x_sample, cache_k, cache_v, cache_ki, state_gla, g_attn, w_in, w_gla_gate2, b_gla_gate,
           g_gla_out, g_q, g_k, g_ki, w_branch_a, w_branch_b, w_out, g_ffn, w_ff_gate, w_ff_up, w_ff_down,
           w_router, w_moe_gate, w_moe_up, w_moe_down):
    h = jnp.concatenate([x_prompt.reshape(N_PROMPT, D_MODEL), x_sample.reshape(N_SAMPLE, D_MODEL)], axis=0)
    pos = jnp.concatenate([jnp.tile(jnp.arange(SEQ, dtype=I32), BATCH),
                           jnp.tile(PAST_LEN + jnp.arange(DEC_SEQ, dtype=I32), DEC_BATCH)])
    tabs = rope_tables(pos)
    kvw = DSA_KV * HEAD_DIM
    l_sample = PAST_LEN + DEC_SEQ
    zeros_state = jnp.zeros((BATCH, GLA_HEADS, GLA_DK, GLA_DV), F32)
    outs = {n: [] for n in ('kp', 'vp', 'kip', 'sp', 'ks', 'vs', 'kis', 'ss')}

    for l in range(DEPTH):
        hn = rmsnorm_cast(h, g_attn[l], TM_DOWN)
        proj = matmul(hn, permute_w_in(w_in[l]), TM_BIG, 512, name="in_proj")
        q_b, k_f, k_b, v_b, qi_b, ki_f, ki_b = dsa_prep(proj, tabs, g_q[l], g_k[l], g_ki[l], 512)

        wg2 = gla_gate_weight(w_gla_gate2[l])
        oa_p, s_p = gla_scan(proj, wg2, b_gla_gate[l], g_gla_out[l], zeros_state, 0, SEQ, CHUNK, TC_GLA)
        oa_s, s_s = gla_scan(proj, wg2, b_gla_gate[l], g_gla_out[l], state_gla[l], N_PROMPT, DEC_SEQ,
                             DEC_SEQ, DEC_SEQ)

        ob_p = dsa_attention(
            q_b, qi_b, proj,
            k_b[:N_PROMPT].reshape(BATCH, SEQ, kvw), v_b[:N_PROMPT].reshape(BATCH, SEQ, 2 * kvw),
            ki_b[:N_PROMPT].reshape(BATCH, SEQ, IDX_DIM),
            row0=0, nq=SEQ // QBLK, qb=QBLK, tk=TK_DSA, n_sel=min(TOPK_MAX, SEQ // 4), l_valid=SEQ,
            causal_blocks=True, qpos_static=0)

        def with_cache(cache, new, width):
            c = cache.reshape(DEC_BATCH, PAST_LEN, width).astype(BF16)
            n = new[N_PROMPT:].reshape(DEC_BATCH, DEC_SEQ, width)
            pad = jnp.zeros((DEC_BATCH, SAMPLE_LP - l_sample, width), BF16)
            return jnp.concatenate([c, n, pad], axis=1)

        ob_s = dsa_attention(
            q_b, qi_b, proj,
            with_cache(cache_k[l], k_b, kvw),
            with_cache(jnp.concatenate([cache_v[l], jnp.ones_like(cache_v[l])], axis=-1), v_b, 2 * kvw),
            with_cache(cache_ki[l], ki_b, IDX_DIM),
            row0=N_PROMPT, nq=1, qb=DEC_SEQ, tk=SAMPLE_LP, n_sel=min(TOPK_MAX, l_sample // 4),
            l_valid=l_sample, causal_blocks=False, qpos_static=PAST_LEN)

        o_a = jnp.concatenate([oa_p, oa_s], axis=0)
        o_b = jnp.concatenate([ob_p, ob_s], axis=0)
        merged = merge_branches(o_a, o_b, w_branch_a[l], w_branch_b[l], proj, TM_BIG, 512)
        h = matmul(merged, w_out[l], TM_BIG, 512, residual=h, name="out_proj")

        j = l // 2
        if l % 2 == 0:
            hn = rmsnorm_cast(h, g_ffn[l], TM_DOWN)
            a = swiglu_up(hn, w_ff_gate[j], w_ff_up[j], TM_BIG, 256)
            h = matmul(a, w_ff_down[j], TM_DOWN, 512, residual=h, name="ffn_down", weight_resident=True)
        else:
            hn, hn_packed = rmsnorm_pack(h, g_ffn[l], TM_DOWN)
            route = moe_router(hn, w_router[j], TM_DOWN)
            dest, tile_expert, n_valid = moe_plan(route, TM_MOE, MOE_ROWS)
            xs = moe_dispatch(hn_packed, dest, MOE_ROWS, TR_MOE)
            a = moe_up(xs, w_moe_gate[j], w_moe_up[j], tile_expert, n_valid, TM_MOE, 512)
            y = moe_down(a, w_moe_down[j], tile_expert, n_valid, TM_MOE, 512)
            h = moe_combine(y, dest, route, h, TR_MOE)

        v_f = proj[:, OFF_DSA_V:OFF_DSA_V + kvw]
        outs['kp'].append(k_f[:N_PROMPT].reshape(BATCH, SEQ, DSA_KV, HEAD_DIM))
        outs['vp'].append(v_f[:N_PROMPT].reshape(BATCH, SEQ, DSA_KV, HEAD_DIM))
        outs['kip'].append(ki_f[:N_PROMPT].reshape(BATCH, SEQ, IDX_DIM))
        outs['sp'].append(s_p)
        outs['ks'].append(k_f[N_PROMPT:].reshape(DEC_BATCH, DEC_SEQ, DSA_KV, HEAD_DIM))
        outs['vs'].append(v_f[N_PROMPT:].reshape(DEC_BATCH, DEC_SEQ, DSA_KV, HEAD_DIM))
        outs['kis'].append(ki_f[N_PROMPT:].reshape(DEC_BATCH, DEC_SEQ, IDX_DIM))
        outs['ss'].append(s_s)

    st = {n: jnp.stack(v) for n, v in outs.items()}
    return (h[:N_PROMPT].reshape(BATCH, SEQ, D_MODEL), h[N_PROMPT:].reshape(DEC_BATCH, DEC_SEQ, D_MODEL),
            st['kp'], st['vp'], st['kip'], st['sp'], st['ks'], st['vs'], st['kis'], st['ss'])
```

```python
import functools

import jax
import jax.numpy as jnp
from jax import lax
from jax.experimental import pallas as pl
from jax.experimental.pallas import tpu as pltpu

F32 = jnp.float32
BF16 = jnp.bfloat16
I32 = jnp.int32

D_MODEL = 2048
BATCH = 2
SEQ = 8192
DEPTH = 2
DEC_BATCH = 32
DEC_SEQ = 16
PAST_LEN = 2048
CHUNK = 64
QBLK = 128
ROPE_THETA = 10000.0
EPS = 1e-6
GLA_HEADS = 4
GLA_DK = 128
GLA_DV = 256
GLA_RANK = 16
GLA_TAU = 16.0
DSA_HEADS = 8
DSA_KV = 2
DSA_GROUP = DSA_HEADS // DSA_KV
HEAD_DIM = 128
IDX_HEADS = 8
IDX_DIM = 64
TOPK_MAX = 256
MIX_A = GLA_HEADS * GLA_DV
MIX_B = DSA_HEADS * HEAD_DIM
D_FF = 5632
N_EXPERTS = 8
TOP_K = 2

N_PROMPT = BATCH * SEQ
N_SAMPLE = DEC_BATCH * DEC_SEQ
N_TOK = N_PROMPT + N_SAMPLE

LANES = 128
VMEM_LIMIT_BYTES = 56 * 1024 * 1024

COL_SIZES = (
    ('gla_q', 512), ('gla_k', 512), ('gla_v', 1024), ('gla_glr', 16), ('gla_r', 1024),
    ('dsa_q', 1024), ('dsa_k', 256), ('dsa_v', 256), ('idx_q', 512), ('idx_k', 64),
    ('idx_w', 8), ('gate_a', 2048), ('gate_b', 2048),
)
PERM_ORDER = ('gla_q', 'gla_k', 'gla_v', 'gla_r', 'dsa_q', 'dsa_k', 'dsa_v', 'idx_q',
              'gate_a', 'gate_b', 'idx_k', 'gla_glr', 'idx_w')
P_COLS = 9728
OFF_GLA_Q, OFF_GLA_K, OFF_GLA_V, OFF_GLA_R = 0, 512, 1024, 2048
OFF_DSA_Q, OFF_DSA_K, OFF_DSA_V, OFF_IDX_Q = 3072, 4096, 4352, 4608
OFF_GATE_A, OFF_GATE_B, OFF_SMALL = 5120, 7168, 9216
SM_IDX_K, SM_GLR, SM_IDX_W = 0, 64, 80

ROUTE_E1, ROUTE_E2, ROUTE_W1, ROUTE_W2 = 0, 1, 2, 3
HI16 = -65536
KEY_NEG_INF = -2139095041
INT_MIN = -2147483648
MASK_BIAS = -1e30
M_INIT = -5e29
Q_SCALE = (HEAD_DIM ** -0.5) * 1.4426950408889634


def _cparams(*sem):
    return pltpu.CompilerParams(dimension_semantics=sem, vmem_limit_bytes=VMEM_LIMIT_BYTES)


def _dot(a, b):
    return jnp.dot(a, b, preferred_element_type=F32)


def _dot_nt(a, b):
    return lax.dot_general(a, b, (((1,), (1,)), ((), ())), preferred_element_type=F32)


def _dot_tn(a, b):
    return lax.dot_general(a, b, (((0,), (0,)), ((), ())), preferred_element_type=F32)


def _split3(a):
    a1 = a.astype(BF16)
    r1 = a - a1.astype(F32)
    a2 = r1.astype(BF16)
    a3 = (r1 - a2.astype(F32)).astype(BF16)
    return a1, a2, a3


def _rmsnorm_kernel(x_ref, g_ref, o_ref):
    x = x_ref[...]
    ms = jnp.mean(x * x, axis=-1, keepdims=True)
    o_ref[...] = ((x * lax.rsqrt(ms + EPS)) * g_ref[...]).astype(BF16)


def rmsnorm_cast(x, g, tm):
    n, d = x.shape
    return pl.pallas_call(
        _rmsnorm_kernel,
        grid=(n // tm,),
        in_specs=[pl.BlockSpec((tm, d), lambda i: (i, 0)), pl.BlockSpec((1, d), lambda i: (0, 0))],
        out_specs=pl.BlockSpec((tm, d), lambda i: (i, 0)),
        out_shape=jax.ShapeDtypeStruct((n, d), BF16),
        compiler_params=_cparams("parallel"),
        name="rmsnorm_cast",
    )(x, g.reshape(1, d))


def _mm_kernel(x_ref, w_ref, o_ref):
    o_ref[...] = _dot(x_ref[...], w_ref[...].astype(BF16))


def _mm_res_kernel(x_ref, w_ref, r_ref, o_ref):
    o_ref[...] = r_ref[...] + _dot(x_ref[...], w_ref[...].astype(BF16))


def matmul(x, w, tm, tn, residual=None, name="matmul", weight_resident=False):
    m, k = x.shape
    n = w.shape[1]
    if weight_resident:
        grid = (n // tn, m // tm)
        rc = lambda a, b: (b, a)
    else:
        grid = (m // tm, n // tn)
        rc = lambda a, b: (a, b)
    in_specs = [pl.BlockSpec((tm, k), lambda a, b: (rc(a, b)[0], 0)),
                pl.BlockSpec((k, tn), lambda a, b: (0, rc(a, b)[1]))]
    args = [x, w]
    body = _mm_kernel
    if residual is not None:
        in_specs.append(pl.BlockSpec((tm, tn), lambda a, b: rc(a, b)))
        args.append(residual)
        body = _mm_res_kernel
    return pl.pallas_call(
        body,
        grid=grid,
        in_specs=in_specs,
        out_specs=pl.BlockSpec((tm, tn), lambda a, b: rc(a, b)),
        out_shape=jax.ShapeDtypeStruct((m, n), F32),
        compiler_params=_cparams("parallel", "parallel"),
        name=name,
    )(*args)


def _merge_kernel(oa_ref, ob_ref, wa_ref, wb_ref, ga_ref, gb_ref, o_ref):
    a = _dot(oa_ref[...], wa_ref[...].astype(BF16))
    b = _dot(ob_ref[...], wb_ref[...].astype(BF16))
    o_ref[...] = (jax.nn.sigmoid(ga_ref[...]) * a + jax.nn.sigmoid(gb_ref[...]) * b).astype(BF16)


def merge_branches(o_a, o_b, w_pa, w_pb, proj, tm, tn):
    m = o_a.shape[0]
    n = w_pa.shape[1]
    ja, jb = OFF_GATE_A // tn, OFF_GATE_B // tn
    return pl.pallas_call(
        _merge_kernel,
        grid=(m // tm, n // tn),
        in_specs=[
            pl.BlockSpec((tm, MIX_A), lambda i, j: (i, 0)),
            pl.BlockSpec((tm, MIX_B), lambda i, j: (i, 0)),
            pl.BlockSpec((MIX_A, tn), lambda i, j: (0, j)),
            pl.BlockSpec((MIX_B, tn), lambda i, j: (0, j)),
            pl.BlockSpec((tm, tn), lambda i, j: (i, ja + j)),
            pl.BlockSpec((tm, tn), lambda i, j: (i, jb + j)),
        ],
        out_specs=pl.BlockSpec((tm, tn), lambda i, j: (i, j)),
        out_shape=jax.ShapeDtypeStruct((m, n), BF16),
        compiler_params=_cparams("parallel", "parallel"),
        name="merge_branches",
    )(o_a, o_b, w_pa, w_pb, proj, proj)


def _swiglu_kernel(x_ref, wg_ref, wu_ref, o_ref):
    x = x_ref[...]
    a = _dot(x, wg_ref[...].astype(BF16))
    b = _dot(x, wu_ref[...].astype(BF16))
    o_ref[...] = (a * jax.nn.sigmoid(a) * b).astype(BF16)


def swiglu_up(x, w_gate, w_up, tm, tn):
    m, k = x.shape
    f = w_gate.shape[1]
    return pl.pallas_call(
        _swiglu_kernel,
        grid=(m // tm, f // tn),
        in_specs=[
            pl.BlockSpec((tm, k), lambda i, j: (i, 0)),
            pl.BlockSpec((k, tn), lambda i, j: (0, j)),
            pl.BlockSpec((k, tn), lambda i, j: (0, j)),
        ],
        out_specs=pl.BlockSpec((tm, tn), lambda i, j: (i, j)),
        out_shape=jax.ShapeDtypeStruct((m, f), BF16),
        compiler_params=_cparams("parallel", "parallel"),
        name="swiglu_up",
    )(x, w_gate, w_up)


def _router_kernel(x_ref, w_ref, o_ref):
    logits = _dot(x_ref[...], w_ref[...].astype(BF16))
    lane = lax.broadcasted_iota(I32, logits.shape, 1)
    logits = jnp.where(lane < N_EXPERTS, logits, -jnp.inf)
    t1 = jnp.max(logits, axis=-1, keepdims=True)
    i1 = jnp.min(jnp.where(logits == t1, lane, LANES), axis=-1, keepdims=True)
    rest = jnp.where(lane == i1, -jnp.inf, logits)
    t2 = jnp.max(rest, axis=-1, keepdims=True)
    i2 = jnp.min(jnp.where(rest == t2, lane, LANES), axis=-1, keepdims=True)
    e2 = jnp.exp(t2 - t1)
    den = 1.0 + e2
    o_ref[...] = jnp.where(lane == ROUTE_E1, i1.astype(F32), 0.0) + jnp.where(lane == ROUTE_E2, i2.astype(F32), 0.0) \
        + jnp.where(lane == ROUTE_W1, 1.0 / den, 0.0) + jnp.where(lane == ROUTE_W2, e2 / den, 0.0)


def moe_router(x, w_router, tm):
    m, k = x.shape
    w_pad = jnp.pad(w_router, ((0, 0), (0, LANES - w_router.shape[1])))
    return pl.pallas_call(
        _router_kernel,
        grid=(m // tm,),
        in_specs=[pl.BlockSpec((tm, k), lambda i: (i, 0)), pl.BlockSpec((k, LANES), lambda i: (0, 0))],
        out_specs=pl.BlockSpec((tm, LANES), lambda i: (i, 0)),
        out_shape=jax.ShapeDtypeStruct((m, LANES), F32),
        compiler_params=_cparams("parallel"),
        name="moe_router",
    )(x, w_pad)


def _rmsnorm_pack_kernel(x_ref, g_ref, o_ref, p_ref):
    x = x_ref[...]
    ms = jnp.mean(x * x, axis=-1, keepdims=True)
    y = ((x * lax.rsqrt(ms + EPS)) * g_ref[...]).astype(BF16)
    o_ref[...] = y
    bits = pltpu.bitcast(y.astype(F32), I32)
    half = x.shape[1] // 2
    p_ref[...] = (bits[:, half:] & HI16) | lax.shift_right_logical(bits[:, :half], 16)


def rmsnorm_pack(x, g, tm):
    n, d = x.shape
    return pl.pallas_call(
        _rmsnorm_pack_kernel,
        grid=(n // tm,),
        in_specs=[pl.BlockSpec((tm, d), lambda i: (i, 0)), pl.BlockSpec((1, d), lambda i: (0, 0))],
        out_specs=[pl.BlockSpec((tm, d), lambda i: (i, 0)), pl.BlockSpec((tm, d // 2), lambda i: (i, 0))],
        out_shape=[jax.ShapeDtypeStruct((n, d), BF16), jax.ShapeDtypeStruct((n, d // 2), I32)],
        compiler_params=_cparams("parallel"),
        name="rmsnorm_pack",
    )(x, g.reshape(1, d))


def _unpack_bf16(p):
    lo = pltpu.bitcast(lax.shift_left(p, 16), F32).astype(BF16)
    hi = pltpu.bitcast(p & HI16, F32).astype(BF16)
    return lo, hi


def moe_plan(route, tm, m_pad):
    n = route.shape[0]
    ef = route[:, ROUTE_E1:ROUTE_E2 + 1].astype(I32).reshape(2 * n)
    onehot = (ef[:, None] == jnp.arange(N_EXPERTS, dtype=I32)[None, :]).astype(I32)
    counts = jnp.sum(onehot, axis=0)
    rank = jnp.sum((jnp.cumsum(onehot, axis=0) - onehot) * onehot, axis=1)
    padded = ((counts + tm - 1) // tm) * tm
    ends = jnp.cumsum(padded)
    dest = jnp.sum(onehot * (ends - padded)[None, :], axis=1) + rank
    tile_start = jnp.arange(m_pad // tm, dtype=I32) * tm
    tile_expert = jnp.minimum(jnp.sum((tile_start[:, None] >= ends[None, :]).astype(I32), axis=1), N_EXPERTS - 1)
    n_valid = (ends[-1] // tm).reshape(1)
    return dest, tile_expert, n_valid


def _dispatch_kernel(dest_ref, x_ref, xs_in_ref, xs_ref, sem):
    del xs_in_ref
    rows = x_ref.shape[0]

    def row_copy(r, s):
        d = dest_ref[0, 0, TOP_K * r + s]
        return pltpu.make_async_copy(x_ref.at[pl.ds(r, 1), :], xs_ref.at[pl.ds(d, 1), :], sem)

    def start(r, c):
        for s in range(TOP_K):
            row_copy(r, s).start()
        return c

    def wait(r, c):
        for s in range(TOP_K):
            row_copy(r, s).wait()
        return c

    lax.fori_loop(0, rows, start, 0)
    lax.fori_loop(0, rows, wait, 0)


def moe_dispatch(xp, dest, m_pad, tr):
    n, w = xp.shape
    return pl.pallas_call(
        _dispatch_kernel,
        grid=(n // tr,),
        in_specs=[
            pl.BlockSpec((1, 1, TOP_K * tr), lambda i: (i, 0, 0), memory_space=pltpu.SMEM),
            pl.BlockSpec((tr, w), lambda i: (i, 0)),
            pl.BlockSpec(memory_space=pl.ANY),
        ],
        out_specs=pl.BlockSpec(memory_space=pl.ANY),
        out_shape=jax.ShapeDtypeStruct((m_pad, w), I32),
        scratch_shapes=[pltpu.SemaphoreType.DMA(())],
        input_output_aliases={2: 0},
        compiler_params=_cparams("arbitrary"),
        name="moe_dispatch",
    )(dest.reshape(n // tr, 1, TOP_K * tr), xp, jnp.zeros((m_pad, w), I32))


def _moe_up_kernel(te_ref, nv_ref, x_ref, wg_ref, wu_ref, o_ref):
    del te_ref
    valid = pl.program_id(1) < nv_ref[0]

    @pl.when(valid)
    def _():
        lo, hi = _unpack_bf16(x_ref[...])
        half = x_ref.shape[1]
        a = _dot(lo, wg_ref[:half, :].astype(BF16)) + _dot(hi, wg_ref[half:, :].astype(BF16))
        b = _dot(lo, wu_ref[:half, :].astype(BF16)) + _dot(hi, wu_ref[half:, :].astype(BF16))
        o_ref[...] = (a * jax.nn.sigmoid(a) * b).astype(BF16)

    @pl.when(jnp.logical_not(valid))
    def _():
        o_ref[...] = jnp.zeros_like(o_ref)


def moe_up(xs, w_gate, w_up, tile_expert, n_valid, tm, tn):
    m = xs.shape[0]
    _, k, f = w_gate.shape
    grid_spec = pltpu.PrefetchScalarGridSpec(
        num_scalar_prefetch=2,
        grid=(f // tn, m // tm),
        in_specs=[
            pl.BlockSpec((tm, k // 2), lambda j, i, te, nv: (i, 0)),
            pl.BlockSpec((None, k, tn), lambda j, i, te, nv: (te[i], 0, j)),
            pl.BlockSpec((None, k, tn), lambda j, i, te, nv: (te[i], 0, j)),
        ],
        out_specs=pl.BlockSpec((tm, tn), lambda j, i, te, nv: (i, j)),
    )
    return pl.pallas_call(
        _moe_up_kernel,
        grid_spec=grid_spec,
        out_shape=jax.ShapeDtypeStruct((m, f), BF16),
        compiler_params=_cparams("parallel", "arbitrary"),
        name="moe_up",
    )(tile_expert, n_valid, xs, w_gate, w_up)


def _moe_down_kernel(te_ref, nv_ref, a_ref, w_ref, o_ref):
    del te_ref
    valid = pl.program_id(1) < nv_ref[0]

    @pl.when(valid)
    def _():
        o_ref[...] = _dot(a_ref[...], w_ref[...].astype(BF16))

    @pl.when(jnp.logical_not(valid))
    def _():
        o_ref[...] = jnp.zeros_like(o_ref)


def moe_down(a, w_down, tile_expert, n_valid, tm, tn):
    m, f = a.shape
    d = w_down.shape[2]
    grid_spec = pltpu.PrefetchScalarGridSpec(
        num_scalar_prefetch=2,
        grid=(d // tn, m // tm),
        in_specs=[
            pl.BlockSpec((tm, f), lambda j, i, te, nv: (i, 0)),
            pl.BlockSpec((None, f, tn), lambda j, i, te, nv: (te[i], 0, j)),
        ],
        out_specs=pl.BlockSpec((tm, tn), lambda j, i, te, nv: (i, j)),
    )
    return pl.pallas_call(
        _moe_down_kernel,
        grid_spec=grid_spec,
        out_shape=jax.ShapeDtypeStruct((m, d), F32),
        compiler_params=_cparams("parallel", "arbitrary"),
        name="moe_down",
    )(tile_expert, n_valid, a, w_down)


def _combine_kernel(dest_ref, y_ref, h_ref, rt_ref, o_ref, buf_ref, sem):
    rows = h_ref.shape[0]

    def row_copy(r, s):
        d = dest_ref[0, 0, TOP_K * r + s]
        return pltpu.make_async_copy(y_ref.at[pl.ds(d, 1), :], buf_ref.at[s, pl.ds(r, 1), :], sem.at[s])

    def start(r, c):
        for s in range(TOP_K):
            row_copy(r, s).start()
        return c

    def wait(r, c):
        for s in range(TOP_K):
            row_copy(r, s).wait()
        return c

    lax.fori_loop(0, rows, start, 0)
    lax.fori_loop(0, rows, wait, 0)
    rt = rt_ref[...]
    o_ref[...] = h_ref[...] + (rt[:, ROUTE_W1:ROUTE_W1 + 1] * buf_ref[0] + rt[:, ROUTE_W2:ROUTE_W2 + 1] * buf_ref[1])


def moe_combine(y, dest, route, h, tr):
    n, d = h.shape
    return pl.pallas_call(
        _combine_kernel,
        grid=(n // tr,),
        in_specs=[
            pl.BlockSpec((1, 1, TOP_K * tr), lambda i: (i, 0, 0), memory_space=pltpu.SMEM),
            pl.BlockSpec(memory_space=pl.ANY),
            pl.BlockSpec((tr, d), lambda i: (i, 0)),
            pl.BlockSpec((tr, LANES), lambda i: (i, 0)),
        ],
        out_specs=pl.BlockSpec((tr, d), lambda i: (i, 0)),
        out_shape=jax.ShapeDtypeStruct((n, d), F32),
        scratch_shapes=[pltpu.VMEM((TOP_K, tr, d), F32), pltpu.SemaphoreType.DMA((TOP_K,))],
        compiler_params=_cparams("arbitrary"),
        name="moe_combine",
    )(dest.reshape(n // tr, 1, TOP_K * tr), y, h, route)


def _prep_kernel(q_ref, k_ref, v_ref, qi_ref, sm_ref, c128_ref, s128_ref, c64_ref, s64_ref,
                 gq_ref, gk_ref, gki_ref,
                 qo_ref, kf_ref, kb_ref, vb_ref, qio_ref, kif_ref, kib_ref):
    c128, s128 = c128_ref[...], s128_ref[...]
    c64, s64 = c64_ref[...], s64_ref[...]
    lane = lax.broadcasted_iota(I32, c64.shape, 1)
    lane_lo = (lane & 63) < 32

    def rope128(x):
        return x * c128 + pltpu.roll(x, 64, 1) * s128

    def rope64(x):
        partner = jnp.where(lane_lo, pltpu.roll(x, 96, 1), pltpu.roll(x, 32, 1))
        return x * c64 + partner * s64

    def norm128(x, g):
        ms = jnp.mean(x * x, axis=-1, keepdims=True)
        return x * lax.rsqrt(ms + EPS) * g

    gq, gk = gq_ref[...], gk_ref[...]
    for h in range(DSA_HEADS):
        sl = slice(h * HEAD_DIM, (h + 1) * HEAD_DIM)
        qo_ref[:, sl] = (rope128(norm128(q_ref[:, sl], gq)) * Q_SCALE).astype(BF16)
    for h in range(DSA_KV):
        sl = slice(h * HEAD_DIM, (h + 1) * HEAD_DIM)
        kr = rope128(norm128(k_ref[:, sl], gk))
        kf_ref[:, sl] = kr
        kb_ref[:, sl] = kr.astype(BF16)
    ones = jnp.ones((v_ref.shape[0], HEAD_DIM), BF16)
    for h in range(DSA_KV):
        vb_ref[:, 2 * h * HEAD_DIM:(2 * h + 1) * HEAD_DIM] = v_ref[:, h * HEAD_DIM:(h + 1) * HEAD_DIM].astype(BF16)
        vb_ref[:, (2 * h + 1) * HEAD_DIM:(2 * h + 2) * HEAD_DIM] = ones
    for t in range(IDX_HEADS * IDX_DIM // LANES):
        sl = slice(t * LANES, (t + 1) * LANES)
        qio_ref[:, sl] = rope64(qi_ref[:, sl]).astype(BF16)
    sm = sm_ref[...]
    ms = jnp.sum(jnp.where(lane < IDX_DIM, sm * sm, 0.0), axis=-1, keepdims=True) * (1.0 / IDX_DIM)
    ki = rope64(sm * lax.rsqrt(ms + EPS) * gki_ref[...])
    kif_ref[...] = ki[:, :IDX_DIM]
    kib_ref[...] = ki[:, :IDX_DIM].astype(BF16)


def dsa_prep(proj, tabs, g_q, g_k, g_ki, tm):
    n = proj.shape[0]
    c128, s128, c64, s64 = tabs
    gki_pad = jnp.pad(g_ki, (0, LANES - IDX_DIM)).reshape(1, LANES)
    row = lambda w, off: pl.BlockSpec((tm, w), lambda i: (i, off // w))
    tab = pl.BlockSpec((tm, LANES), lambda i: (i, 0))
    vec = pl.BlockSpec((1, LANES), lambda i: (0, 0))
    out = lambda w: pl.BlockSpec((tm, w), lambda i: (i, 0))
    kvw = DSA_KV * HEAD_DIM
    return pl.pallas_call(
        _prep_kernel,
        grid=(n // tm,),
        in_specs=[row(MIX_B, OFF_DSA_Q), row(kvw, OFF_DSA_K), row(kvw, OFF_DSA_V),
                  row(IDX_HEADS * IDX_DIM, OFF_IDX_Q), row(LANES, OFF_SMALL),
                  tab, tab, tab, tab, vec, vec, vec],
        out_specs=[out(MIX_B), out(kvw), out(kvw), out(2 * kvw), out(IDX_HEADS * IDX_DIM),
                   out(IDX_DIM), out(IDX_DIM)],
        out_shape=[
            jax.ShapeDtypeStruct((n, MIX_B), BF16),
            jax.ShapeDtypeStruct((n, kvw), F32),
            jax.ShapeDtypeStruct((n, kvw), BF16),
            jax.ShapeDtypeStruct((n, 2 * kvw), BF16),
            jax.ShapeDtypeStruct((n, IDX_HEADS * IDX_DIM), BF16),
            jax.ShapeDtypeStruct((n, IDX_DIM), F32),
            jax.ShapeDtypeStruct((n, IDX_DIM), BF16),
        ],
        compiler_params=_cparams("parallel"),
        name="dsa_prep",
    )(proj, proj, proj, proj, proj, c128, s128, c64, s64,
      g_q.reshape(1, LANES), g_k.reshape(1, LANES), gki_pad)


def rope_tables(pos):
    def tab(half, reps):
        inv = ROPE_THETA ** (-jnp.arange(half, dtype=F32) / half)
        ang = pos.astype(F32)[:, None] * inv[None, :]
        cos, sin = jnp.cos(ang), jnp.sin(ang)
        return jnp.tile(jnp.concatenate([cos, cos], -1), (1, reps)), jnp.tile(jnp.concatenate([-sin, sin], -1), (1, reps))
    c128, s128 = tab(HEAD_DIM // 2, 1)
    c64, s64 = tab(IDX_DIM // 2, 2)
    return c128, s128, c64, s64


def _gla_kernel(q_ref, k_ref, v_ref, r_ref, sm_ref, wg2_ref, bg_ref, gout_ref, s0_ref,
                o_ref, sout_ref, st_ref, *, chunk, n_chunks):
    t = pl.program_id(1)

    @pl.when(t == 0)
    def _():
        for h in range(GLA_HEADS):
            st_ref[h] = s0_ref[h].T

    row = lax.broadcasted_iota(I32, (chunk, chunk), 0)
    col = lax.broadcasted_iota(I32, (chunk, chunk), 1)
    causal = row >= col
    tril = jnp.where(causal, 1.0, 0.0).astype(BF16)
    gout = gout_ref[...]

    heads = range(GLA_HEADS)
    dks = [slice(h * GLA_DK, (h + 1) * GLA_DK) for h in heads]
    dvs = [slice(h * GLA_DV, (h + 1) * GLA_DV) for h in heads]
    ws = [_split3(wg2_ref[h]) for h in heads]
    bgs = [bg_ref[:, dks[h]] for h in heads]
    sts = [st_ref[h] for h in heads]
    for c in range(n_chunks):
        sl = pl.ds(c * chunk, chunk)
        s1, s2, s3 = _split3(sm_ref[sl, :])
        qs = [q_ref[sl, dks[h]] * (GLA_DK ** -0.5) for h in heads]
        ks = [k_ref[sl, dks[h]] for h in heads]
        vs = [v_ref[sl, dvs[h]].astype(BF16) for h in heads]
        rs = [r_ref[sl, dvs[h]] for h in heads]
        zs = [(_dot(s1, w[0]) + (_dot(s1, w[1]) + _dot(s2, w[0]))
               + (_dot(s1, w[2]) + _dot(s2, w[1]) + _dot(s3, w[0]))) + bgs[h] for h, w in enumerate(ws)]
        gs = [_split3((jnp.minimum(z, 0.0) - jnp.log1p(jnp.exp(-jnp.abs(z)))) * (1.0 / GLA_TAU)) for z in zs]
        bs = [_dot(tril, g[0]) + _dot(tril, g[1]) + _dot(tril, g[2]) for g in gs]
        b_last = [b[chunk - 1:chunk, :] for b in bs]
        qe = [(qs[h] * jnp.exp(bs[h])).astype(BF16) for h in heads]
        ke = [(ks[h] * jnp.exp(-bs[h])).astype(BF16) for h in heads]
        kd = [(ks[h] * jnp.exp(b_last[h] - bs[h])).astype(BF16) for h in heads]
        a = [jnp.where(causal, _dot_nt(qe[h], ke[h]), 0.0).astype(BF16) for h in heads]
        o = [_dot(a[h], vs[h]) + _dot_nt(qe[h], sts[h].astype(BF16)) for h in heads]
        sts = [jnp.exp(b_last[h]) * sts[h] + _dot_tn(vs[h], kd[h]) for h in heads]
        for h in heads:
            ms = jnp.mean(o[h] * o[h], axis=-1, keepdims=True)
            on = o[h] * lax.rsqrt(ms + EPS) * gout
            o_ref[sl, dvs[h]] = (rs[h] * jax.nn.sigmoid(rs[h]) * on).astype(BF16)
    for h in heads:
        st_ref[h] = sts[h]

    @pl.when(t == pl.num_programs(1) - 1)
    def _():
        for h in range(GLA_HEADS):
            sout_ref[h] = st_ref[h].T


def gla_scan(proj, wg2_pad, b_g, g_out, s0, row0, seq, chunk, tc):
    nb = s0.shape[0]
    nt = seq // tc
    rb0 = row0 // tc
    hk, hv = GLA_HEADS * GLA_DK, GLA_HEADS * GLA_DV
    rows = lambda w, off: pl.BlockSpec((tc, w), lambda b, t: (rb0 + b * nt + t, off // w))
    state = pl.BlockSpec((None, GLA_HEADS, GLA_DK, GLA_DV), lambda b, t: (b, 0, 0, 0))
    kern = functools.partial(_gla_kernel, chunk=chunk, n_chunks=tc // chunk)
    return pl.pallas_call(
        kern,
        grid=(nb, nt),
        in_specs=[
            rows(hk, OFF_GLA_Q), rows(hk, OFF_GLA_K), rows(hv, OFF_GLA_V), rows(hv, OFF_GLA_R),
            rows(LANES, OFF_SMALL),
            pl.BlockSpec((GLA_HEADS, LANES, GLA_DK), lambda b, t: (0, 0, 0)),
            pl.BlockSpec((1, hk), lambda b, t: (0, 0)),
            pl.BlockSpec((1, GLA_DV), lambda b, t: (0, 0)),
            state,
        ],
        out_specs=[pl.BlockSpec((tc, hv), lambda b, t: (b * nt + t, 0)), state],
        out_shape=[
            jax.ShapeDtypeStruct((nb * seq, MIX_A), BF16),
            jax.ShapeDtypeStruct((nb, GLA_HEADS, GLA_DK, GLA_DV), F32),
        ],
        scratch_shapes=[pltpu.VMEM((GLA_HEADS, GLA_DV, GLA_DK), F32)],
        compiler_params=_cparams("parallel", "arbitrary"),
        name="gla_scan",
    )(proj, proj, proj, proj, proj, wg2_pad, b_g.reshape(1, -1), g_out.reshape(1, -1), s0)


def gla_gate_weight(w_g2):
    w = w_g2.reshape(GLA_RANK, GLA_HEADS, GLA_DK).transpose(1, 0, 2)
    return jnp.pad(w, ((0, 0), (SM_GLR, LANES - SM_GLR - GLA_RANK), (0, 0)))


def _dsa_kernel(q_ref, qi_ref, sm_ref, k_ref, v_ref, ki_ref, o_ref,
                key_scr, vec_scr, qis_scr, qg_scr, m_scr, a_scr, p_scr, acc_scr,
                *, qb, tk, lp, n_sel, l_valid, causal_blocks, qpos_static):
    gq = DSA_GROUP * qb
    if causal_blocks:
        i = pl.program_id(1)
        qpos0 = i * qb
        nt = ((i + 1) * qb + tk - 1) // tk
    else:
        qpos0 = qpos_static
        nt = lp // tk
    nsub = tk // LANES
    idx_bits = max(1, (lp - 1).bit_length())

    for h in range(IDX_HEADS):
        qis_scr[h * qb:(h + 1) * qb, :] = qi_ref[:, h * IDX_DIM:(h + 1) * IDX_DIM]
    for g in range(DSA_KV):
        for hh in range(DSA_GROUP):
            hd = g * DSA_GROUP + hh
            qg_scr[g, hh * qb:(hh + 1) * qb, :] = q_ref[:, hd * HEAD_DIM:(hd + 1) * HEAD_DIM]
    wq = sm_ref[:, SM_IDX_W:SM_IDX_W + IDX_HEADS] * ((IDX_DIM ** -0.5) * (IDX_HEADS ** -0.5))
    w_cols = [wq[:, h:h + 1] for h in range(IDX_HEADS)]
    row_chunk = (qpos0 + lax.broadcasted_iota(I32, (qb, tk), 0)) >> 6
    lane_t = lax.broadcasted_iota(I32, (qb, tk), 1)

    def p1(j, carry):
        ks = pl.multiple_of(j * tk, tk)
        s_all = _dot_nt(qis_scr[...], ki_ref[pl.ds(ks, tk), :])
        acc = jnp.zeros((qb, tk), F32)
        for h in range(IDX_HEADS):
            acc = acc + jnp.maximum(s_all[h * qb:(h + 1) * qb, :], 0.0) * w_cols[h]
        kpos = ks + lane_t
        adm = jnp.where((kpos >> 6) <= row_chunk, kpos, l_valid) < l_valid
        score = jnp.where(adm, acc, -jnp.inf)
        bits = pltpu.bitcast(score, I32)
        key_scr[:, pl.ds(ks, tk)] = bits ^ ((bits >> 31) & 0x7FFFFFFF)
        return carry

    lax.fori_loop(0, nt, p1, 0)

    rh = min(qb, 128)

    row_groups = list(range(0, qb, rh))

    def count(pred, *row_vecs):
        for i, v in enumerate(row_vecs):
            vec_scr[i] = jnp.broadcast_to(v, (qb, LANES))

        def body(j, accs):
            accs = list(accs)
            for gi, r0 in enumerate(row_groups):
                vecs = [vec_scr[i, r0:r0 + rh, :] for i in range(len(row_vecs))]
                for c in range(nsub):
                    off = pl.multiple_of(j * tk + c * LANES, LANES)
                    blk = key_scr[r0:r0 + rh, pl.ds(off, LANES)]
                    idx = off + lax.broadcasted_iota(I32, (rh, LANES), 1)
                    accs[gi] = accs[gi] + jnp.where(pred(blk, idx, *vecs), 1, 0)
            return tuple(accs)

        accs = lax.fori_loop(0, nt, body, tuple(jnp.zeros((rh, LANES), I32) for _ in row_groups))
        acc = accs[0] if len(accs) == 1 else jnp.concatenate(accs, axis=0)
        return jnp.sum(acc, axis=1, keepdims=True)

    c0 = count(lambda blk, idx: blk >= 0)
    thr = jnp.where(c0 >= n_sel, 0, INT_MIN).astype(I32)

    def bit_step(it, thr):
        cand = thr + lax.shift_left(jnp.int32(1), 30 - it)
        cnt = count(lambda blk, idx, c: blk >= c, cand)
        return jnp.where(cnt >= n_sel, cand, thr)

    thr = lax.fori_loop(0, 31, bit_step, thr)
    cnt_ge = count(lambda blk, idx, t: blk >= t, thr)
    tie_rows = jnp.where(thr > KEY_NEG_INF, cnt_ge, 0) > n_sel
    any_tie = jnp.max(jnp.where(tie_rows, 1, 0)) > 0

    def tie_cut():
        need = n_sel - count(lambda blk, idx, t: blk > t, thr)
        pos = jnp.zeros((qb, 1), I32)
        for b in range(idx_bits - 1, -1, -1):
            cand = pos + (1 << b)
            f = count(lambda blk, idx, t, c: jnp.where(blk == t, idx, lp) < c, thr, cand)
            pos = jnp.where(f < need, cand, pos)
        return pos

    cut = lax.cond(any_tie, tie_cut, lambda: jnp.full((qb, 1), lp, I32))

    m_scr[...] = jnp.full(m_scr.shape, M_INIT, F32)
    acc_scr[...] = jnp.zeros(acc_scr.shape, F32)
    rb = min(qb, 32)

    def p3(j, carry):
        ks = pl.multiple_of(j * tk, tk)
        key = key_scr[:, pl.ds(ks, tk)]
        idx = ks + lane_t
        eq_ok = jnp.where(key == thr, idx, lp + 1) <= cut
        bias = jnp.where(key > thr, 0.0, jnp.where(eq_ok, 0.0, MASK_BIAS))
        bias = jnp.where(key > KEY_NEG_INF, bias, MASK_BIAS)
        s_all = [_dot_nt(qg_scr[g], k_ref[pl.ds(ks, tk), g * HEAD_DIM:(g + 1) * HEAD_DIM])
                 for g in range(DSA_KV)]
        for g in range(DSA_KV):
            for r0 in range(0, gq, rb):
                rs = slice(r0, r0 + rb)
                s = s_all[g][rs, :] + bias[r0 % qb:r0 % qb + rb, :]
                m_old = m_scr[g, rs, :]
                m_new = jnp.maximum(m_old, jnp.max(s, axis=1, keepdims=True))
                p_scr[g, rs, :] = jnp.exp2(s - m_new).astype(BF16)
                a_scr[g, rs, :] = jnp.exp2(m_old - m_new)
                m_scr[g, rs, :] = m_new
        for g in range(DSA_KV):
            vt = v_ref[pl.ds(ks, tk), 2 * g * HEAD_DIM:(2 * g + 2) * HEAD_DIM]
            acc_scr[g] = a_scr[g] * acc_scr[g] + _dot(p_scr[g], vt)
        return carry

    lax.fori_loop(0, nt, p3, 0)

    for g in range(DSA_KV):
        acc = acc_scr[g]
        o = acc[:, :HEAD_DIM] / acc[:, HEAD_DIM:]
        for hh in range(DSA_GROUP):
            hd = g * DSA_GROUP + hh
            o_ref[:, hd * HEAD_DIM:(hd + 1) * HEAD_DIM] = o[hh * qb:(hh + 1) * qb, :].astype(BF16)


def dsa_attention(q, qi, proj, k, v, ki, *, row0, nq, qb, tk, n_sel, l_valid, causal_blocks, qpos_static):
    nb, lp, kvw = k.shape
    rb0 = row0 // qb
    gq = DSA_GROUP * qb
    kern = functools.partial(_dsa_kernel, qb=qb, tk=tk, lp=lp, n_sel=n_sel, l_valid=l_valid,
                             causal_blocks=causal_blocks, qpos_static=qpos_static)
    return pl.pallas_call(
        kern,
        grid=(nb, nq),
        in_specs=[
            pl.BlockSpec((qb, MIX_B), lambda b, i: (rb0 + b * nq + i, 0)),
            pl.BlockSpec((qb, IDX_HEADS * IDX_DIM), lambda b, i: (rb0 + b * nq + i, 0)),
            pl.BlockSpec((qb, LANES), lambda b, i: (rb0 + b * nq + i, OFF_SMALL // LANES)),
            pl.BlockSpec((None, lp, kvw), lambda b, i: (b, 0, 0)),
            pl.BlockSpec((None, lp, 2 * kvw), lambda b, i: (b, 0, 0)),
            pl.BlockSpec((None, lp, IDX_DIM), lambda b, i: (b, 0, 0)),
        ],
        out_specs=pl.BlockSpec((qb, MIX_B), lambda b, i: (b * nq + i, 0)),
        out_shape=jax.ShapeDtypeStruct((nb * nq * qb, MIX_B), BF16),
        scratch_shapes=[
            pltpu.VMEM((qb, lp), I32),
            pltpu.VMEM((2, qb, LANES), I32),
            pltpu.VMEM((IDX_HEADS * qb, IDX_DIM), BF16),
            pltpu.VMEM((DSA_KV, gq, HEAD_DIM), BF16),
            pltpu.VMEM((DSA_KV, gq, 1), F32),
            pltpu.VMEM((DSA_KV, gq, 1), F32),
            pltpu.VMEM((DSA_KV, gq, tk), BF16),
            pltpu.VMEM((DSA_KV, gq, 2 * HEAD_DIM), F32),
        ],
        compiler_params=_cparams("parallel", "arbitrary"),
        name="dsa_attention",
    )(q, qi, proj, k, v, ki)


def permute_w_in(w_in):
    offs, o = {}, 0
    for name, size in COL_SIZES:
        offs[name] = (o, size)
        o += size
    parts = [w_in[:, offs[n][0]:offs[n][0] + offs[n][1]] for n in PERM_ORDER]
    used = sum(offs[n][1] for n in PERM_ORDER)
    parts.append(jnp.zeros((w_in.shape[0], P_COLS - used), w_in.dtype))
    return jnp.concatenate(parts, axis=1)


TM_BIG = 1536
TM_DOWN = 768
TC_GLA = 256
TK_DSA = 512
QB_DSA = 256
SAMPLE_LP = 2176
TM_MOE = 512
TR_MOE = 256
MOE_ROWS = -(-(TOP_K * N_TOK + N_EXPERTS * (TM_MOE - 1)) // TM_MOE) * TM_MOE


def kernel(x_prompt, x_sample, cache_k, cache_v, cache_ki, state_gla, g_attn, w_in, w_gla_gate2, b_gla_gate,
           g_gla_out, g_q, g_k, g_ki, w_branch_a, w_branch_b, w_out, g_ffn, w_ff_gate, w_ff_up, w_ff_down,
           w_router, w_moe_gate, w_moe_up, w_moe_down):
    h = jnp.concatenate([x_prompt.reshape(N_PROMPT, D_MODEL), x_sample.reshape(N_SAMPLE, D_MODEL)], axis=0)
    pos = jnp.concatenate([jnp.tile(jnp.arange(SEQ, dtype=I32), BATCH),
                           jnp.tile(PAST_LEN + jnp.arange(DEC_SEQ, dtype=I32), DEC_BATCH)])
    tabs = rope_tables(pos)
    kvw = DSA_KV * HEAD_DIM
    l_sample = PAST_LEN + DEC_SEQ
    zeros_state = jnp.zeros((BATCH, GLA_HEADS, GLA_DK, GLA_DV), F32)
    outs = {n: [] for n in ('kp', 'vp', 'kip', 'sp', 'ks', 'vs', 'kis', 'ss')}

    for l in range(DEPTH):
        hn = rmsnorm_cast(h, g_attn[l], TM_DOWN)
        proj = matmul(hn, permute_w_in(w_in[l]), TM_BIG, 512, name="in_proj")
        q_b, k_f, k_b, v_b, qi_b, ki_f, ki_b = dsa_prep(proj, tabs, g_q[l], g_k[l], g_ki[l], 512)

        wg2 = gla_gate_weight(w_gla_gate2[l])
        oa_p, s_p = gla_scan(proj, wg2, b_gla_gate[l], g_gla_out[l], zeros_state, 0, SEQ, CHUNK, TC_GLA)
        oa_s, s_s = gla_scan(proj, wg2, b_gla_gate[l], g_gla_out[l], state_gla[l], N_PROMPT, DEC_SEQ,
                             DEC_SEQ, DEC_SEQ)

        ob_p = dsa_attention(
            q_b, qi_b, proj,
            k_b[:N_PROMPT].reshape(BATCH, SEQ, kvw), v_b[:N_PROMPT].reshape(BATCH, SEQ, 2 * kvw),
            ki_b[:N_PROMPT].reshape(BATCH, SEQ, IDX_DIM),
            row0=0, nq=SEQ // QB_DSA, qb=QB_DSA, tk=TK_DSA, n_sel=min(TOPK_MAX, SEQ // 4), l_valid=SEQ,
            causal_blocks=True, qpos_static=0)

        def with_cache(cache_parts, new, width):
            c = jnp.concatenate([p.astype(BF16) for p in cache_parts], axis=-1)
            n = new[N_PROMPT:].reshape(DEC_BATCH, DEC_SEQ, width)
            pad = jnp.zeros((DEC_BATCH, SAMPLE_LP - l_sample, width), BF16)
            return jnp.concatenate([c, n, pad], axis=1)

        ck, cv = cache_k[l], cache_v[l]
        ones = jnp.ones((DEC_BATCH, PAST_LEN, HEAD_DIM), BF16)
        ob_s = dsa_attention(
            q_b, qi_b, proj,
            with_cache([ck[:, :, g, :] for g in range(DSA_KV)], k_b, kvw),
            with_cache([p for g in range(DSA_KV) for p in (cv[:, :, g, :], ones)], v_b, 2 * kvw),
            with_cache([cache_ki[l]], ki_b, IDX_DIM),
            row0=N_PROMPT, nq=1, qb=DEC_SEQ, tk=SAMPLE_LP, n_sel=min(TOPK_MAX, l_sample // 4),
            l_valid=l_sample, causal_blocks=False, qpos_static=PAST_LEN)

        o_a = jnp.concatenate([oa_p, oa_s], axis=0)
        o_b = jnp.concatenate([ob_p, ob_s], axis=0)
        merged = merge_branches(o_a, o_b, w_branch_a[l], w_branch_b[l], proj, TM_BIG, 512)
        h = matmul(merged, w_out[l], TM_BIG, 512, residual=h, name="out_proj")

        j = l // 2
        if l % 2 == 0:
            hn = rmsnorm_cast(h, g_ffn[l], TM_DOWN)
            a = swiglu_up(hn, w_ff_gate[j], w_ff_up[j], TM_BIG, 256)
            h = matmul(a, w_ff_down[j], TM_DOWN, 512, residual=h, name="ffn_down", weight_resident=True)
        else:
            hn, hn_packed = rmsnorm_pack(h, g_ffn[l], TM_DOWN)
            route = moe_router(hn, w_router[j], TM_DOWN)
            dest, tile_expert, n_valid = moe_plan(route, TM_MOE, MOE_ROWS)
            xs = moe_dispatch(hn_packed, dest, MOE_ROWS, TR_MOE)
            a = moe_up(xs, w_moe_gate[j], w_moe_up[j], tile_expert, n_valid, TM_MOE, 512)
            y = moe_down(a, w_moe_down[j], tile_expert, n_valid, TM_MOE, 512)
            h = moe_combine(y, dest, route, h, TR_MOE)

        v_f = proj[:, OFF_DSA_V:OFF_DSA_V + kvw]
        outs['kp'].append(k_f[:N_PROMPT].reshape(BATCH, SEQ, DSA_KV, HEAD_DIM))
        outs['vp'].append(v_f[:N_PROMPT].reshape(BATCH, SEQ, DSA_KV, HEAD_DIM))
        outs['kip'].append(ki_f[:N_PROMPT].reshape(BATCH, SEQ, IDX_DIM))
        outs['sp'].append(s_p)
        outs['ks'].append(k_f[N_PROMPT:].reshape(DEC_BATCH, DEC_SEQ, DSA_KV, HEAD_DIM))
        outs['vs'].append(v_f[N_PROMPT:].reshape(DEC_BATCH, DEC_SEQ, DSA_KV, HEAD_DIM))
        outs['kis'].append(ki_f[N_PROMPT:].reshape(DEC_BATCH, DEC_SEQ, IDX_DIM))
        outs['ss'].append(s_s)

    st = {n: jnp.stack(v) for n, v in outs.items()}
    return (h[:N_PROMPT].reshape(BATCH, SEQ, D_MODEL), h[N_PROMPT:].reshape(DEC_BATCH, DEC_SEQ, D_MODEL),
            st['kp'], st['vp'], st['kip'], st['sp'], st['ks'], st['vs'], st['kis'], st['ss'])
```

```python
import functools

import jax
import jax.numpy as jnp
from jax import lax
from jax.experimental import pallas as pl
from jax.experimental.pallas import tpu as pltpu

F32 = jnp.float32
BF16 = jnp.bfloat16
I32 = jnp.int32
I16 = jnp.int16

D_MODEL = 2048
BATCH = 2
SEQ = 8192
DEPTH = 2
DEC_BATCH = 32
DEC_SEQ = 16
PAST_LEN = 2048
CHUNK = 64
QBLK = 128
ROPE_THETA = 10000.0
EPS = 1e-6
GLA_HEADS = 4
GLA_DK = 128
GLA_DV = 256
GLA_RANK = 16
GLA_TAU = 16.0
DSA_HEADS = 8
DSA_KV = 2
DSA_GROUP = DSA_HEADS // DSA_KV
HEAD_DIM = 128
IDX_HEADS = 8
IDX_DIM = 64
TOPK_MAX = 256
MIX_A = GLA_HEADS * GLA_DV
MIX_B = DSA_HEADS * HEAD_DIM
D_FF = 5632
N_EXPERTS = 8
TOP_K = 2

N_PROMPT = BATCH * SEQ
N_SAMPLE = DEC_BATCH * DEC_SEQ
N_TOK = N_PROMPT + N_SAMPLE

LANES = 128
VMEM_LIMIT_BYTES = 56 * 1024 * 1024

COL_SIZES = (
    ('gla_q', 512), ('gla_k', 512), ('gla_v', 1024), ('gla_glr', 16), ('gla_r', 1024),
    ('dsa_q', 1024), ('dsa_k', 256), ('dsa_v', 256), ('idx_q', 512), ('idx_k', 64),
    ('idx_w', 8), ('gate_a', 2048), ('gate_b', 2048),
)
PERM_ORDER = ('gla_q', 'gla_k', 'gla_v', 'gla_r', 'dsa_q', 'dsa_k', 'dsa_v', 'idx_q',
              'gate_a', 'gate_b', 'idx_k', 'gla_glr', 'idx_w')
P_COLS = 9728
OFF_GLA_Q, OFF_GLA_K, OFF_GLA_V, OFF_GLA_R = 0, 512, 1024, 2048
OFF_DSA_Q, OFF_DSA_K, OFF_DSA_V, OFF_IDX_Q = 3072, 4096, 4352, 4608
OFF_GATE_A, OFF_GATE_B, OFF_SMALL = 5120, 7168, 9216
SM_IDX_K, SM_GLR, SM_IDX_W = 0, 64, 80

ROUTE_E1, ROUTE_E2, ROUTE_W1, ROUTE_W2 = 0, 1, 2, 3
HI16 = -65536
KEY_NEG_INF = -2139095041
INT_MIN = -2147483648
HALF_MIN = -32768
MASK_BIAS = -1e30
M_INIT = -5e29
Q_SCALE = (HEAD_DIM ** -0.5) * 1.4426950408889634


def _cparams(*sem):
    return pltpu.CompilerParams(dimension_semantics=sem, vmem_limit_bytes=VMEM_LIMIT_BYTES)


def _dot(a, b):
    return jnp.dot(a, b, preferred_element_type=F32)


def _dot_nt(a, b):
    return lax.dot_general(a, b, (((1,), (1,)), ((), ())), preferred_element_type=F32)


def _dot_tn(a, b):
    return lax.dot_general(a, b, (((0,), (0,)), ((), ())), preferred_element_type=F32)


def _split3(a):
    a1 = a.astype(BF16)
    r1 = a - a1.astype(F32)
    a2 = r1.astype(BF16)
    a3 = (r1 - a2.astype(F32)).astype(BF16)
    return a1, a2, a3


def _rmsnorm_kernel(x_ref, g_ref, o_ref):
    x = x_ref[...]
    ms = jnp.mean(x * x, axis=-1, keepdims=True)
    o_ref[...] = ((x * lax.rsqrt(ms + EPS)) * g_ref[...]).astype(BF16)


def rmsnorm_cast(x, g, tm):
    n, d = x.shape
    return pl.pallas_call(
        _rmsnorm_kernel,
        grid=(n // tm,),
        in_specs=[pl.BlockSpec((tm, d), lambda i: (i, 0)), pl.BlockSpec((1, d), lambda i: (0, 0))],
        out_specs=pl.BlockSpec((tm, d), lambda i: (i, 0)),
        out_shape=jax.ShapeDtypeStruct((n, d), BF16),
        compiler_params=_cparams("parallel"),
        name="rmsnorm_cast",
    )(x, g.reshape(1, d))


def _mm_kernel(x_ref, w_ref, o_ref):
    o_ref[...] = _dot(x_ref[...], w_ref[...].astype(BF16))


def _mm_res_kernel(x_ref, w_ref, r_ref, o_ref):
    o_ref[...] = r_ref[...] + _dot(x_ref[...], w_ref[...].astype(BF16))


def matmul(x, w, tm, tn, residual=None, name="matmul", weight_resident=False):
    m, k = x.shape
    n = w.shape[1]
    if weight_resident:
        grid = (n // tn, m // tm)
        rc = lambda a, b: (b, a)
    else:
        grid = (m // tm, n // tn)
        rc = lambda a, b: (a, b)
    in_specs = [pl.BlockSpec((tm, k), lambda a, b: (rc(a, b)[0], 0)),
                pl.BlockSpec((k, tn), lambda a, b: (0, rc(a, b)[1]))]
    args = [x, w]
    body = _mm_kernel
    if residual is not None:
        in_specs.append(pl.BlockSpec((tm, tn), lambda a, b: rc(a, b)))
        args.append(residual)
        body = _mm_res_kernel
    return pl.pallas_call(
        body,
        grid=grid,
        in_specs=in_specs,
        out_specs=pl.BlockSpec((tm, tn), lambda a, b: rc(a, b)),
        out_shape=jax.ShapeDtypeStruct((m, n), F32),
        compiler_params=_cparams("parallel", "parallel"),
        name=name,
    )(*args)


def _merge_kernel(oa_ref, ob_ref, wa_ref, wb_ref, ga_ref, gb_ref, o_ref):
    a = _dot(oa_ref[...], wa_ref[...].astype(BF16))
    b = _dot(ob_ref[...], wb_ref[...].astype(BF16))
    o_ref[...] = (jax.nn.sigmoid(ga_ref[...]) * a + jax.nn.sigmoid(gb_ref[...]) * b).astype(BF16)


def merge_branches(o_a, o_b, w_pa, w_pb, proj, tm, tn):
    m = o_a.shape[0]
    n = w_pa.shape[1]
    ja, jb = OFF_GATE_A // tn, OFF_GATE_B // tn
    return pl.pallas_call(
        _merge_kernel,
        grid=(m // tm, n // tn),
        in_specs=[
            pl.BlockSpec((tm, MIX_A), lambda i, j: (i, 0)),
            pl.BlockSpec((tm, MIX_B), lambda i, j: (i, 0)),
            pl.BlockSpec((MIX_A, tn), lambda i, j: (0, j)),
            pl.BlockSpec((MIX_B, tn), lambda i, j: (0, j)),
            pl.BlockSpec((tm, tn), lambda i, j: (i, ja + j)),
            pl.BlockSpec((tm, tn), lambda i, j: (i, jb + j)),
        ],
        out_specs=pl.BlockSpec((tm, tn), lambda i, j: (i, j)),
        out_shape=jax.ShapeDtypeStruct((m, n), BF16),
        compiler_params=_cparams("parallel", "parallel"),
        name="merge_branches",
    )(o_a, o_b, w_pa, w_pb, proj, proj)


def _swiglu_kernel(x_ref, wg_ref, wu_ref, o_ref):
    x = x_ref[...]
    a = _dot(x, wg_ref[...].astype(BF16))
    b = _dot(x, wu_ref[...].astype(BF16))
    o_ref[...] = (a * jax.nn.sigmoid(a) * b).astype(BF16)


def swiglu_up(x, w_gate, w_up, tm, tn):
    m, k = x.shape
    f = w_gate.shape[1]
    return pl.pallas_call(
        _swiglu_kernel,
        grid=(m // tm, f // tn),
        in_specs=[
            pl.BlockSpec((tm, k), lambda i, j: (i, 0)),
            pl.BlockSpec((k, tn), lambda i, j: (0, j)),
            pl.BlockSpec((k, tn), lambda i, j: (0, j)),
        ],
        out_specs=pl.BlockSpec((tm, tn), lambda i, j: (i, j)),
        out_shape=jax.ShapeDtypeStruct((m, f), BF16),
        compiler_params=_cparams("parallel", "parallel"),
        name="swiglu_up",
    )(x, w_gate, w_up)


def _router_kernel(x_ref, w_ref, o_ref):
    logits = _dot(x_ref[...], w_ref[...].astype(BF16))
    lane = lax.broadcasted_iota(I32, logits.shape, 1)
    logits = jnp.where(lane < N_EXPERTS, logits, -jnp.inf)
    t1 = jnp.max(logits, axis=-1, keepdims=True)
    i1 = jnp.min(jnp.where(logits == t1, lane, LANES), axis=-1, keepdims=True)
    rest = jnp.where(lane == i1, -jnp.inf, logits)
    t2 = jnp.max(rest, axis=-1, keepdims=True)
    i2 = jnp.min(jnp.where(rest == t2, lane, LANES), axis=-1, keepdims=True)
    e2 = jnp.exp(t2 - t1)
    den = 1.0 + e2
    o_ref[...] = jnp.where(lane == ROUTE_E1, i1.astype(F32), 0.0) + jnp.where(lane == ROUTE_E2, i2.astype(F32), 0.0) \
        + jnp.where(lane == ROUTE_W1, 1.0 / den, 0.0) + jnp.where(lane == ROUTE_W2, e2 / den, 0.0)


def moe_router(x, w_router, tm):
    m, k = x.shape
    w_pad = jnp.pad(w_router, ((0, 0), (0, LANES - w_router.shape[1])))
    return pl.pallas_call(
        _router_kernel,
        grid=(m // tm,),
        in_specs=[pl.BlockSpec((tm, k), lambda i: (i, 0)), pl.BlockSpec((k, LANES), lambda i: (0, 0))],
        out_specs=pl.BlockSpec((tm, LANES), lambda i: (i, 0)),
        out_shape=jax.ShapeDtypeStruct((m, LANES), F32),
        compiler_params=_cparams("parallel"),
        name="moe_router",
    )(x, w_pad)


def _rmsnorm_pack_kernel(x_ref, g_ref, o_ref, p_ref):
    x = x_ref[...]
    ms = jnp.mean(x * x, axis=-1, keepdims=True)
    y = ((x * lax.rsqrt(ms + EPS)) * g_ref[...]).astype(BF16)
    o_ref[...] = y
    bits = pltpu.bitcast(y.astype(F32), I32)
    half = x.shape[1] // 2
    p_ref[...] = (bits[:, half:] & HI16) | lax.shift_right_logical(bits[:, :half], 16)


def rmsnorm_pack(x, g, tm):
    n, d = x.shape
    return pl.pallas_call(
        _rmsnorm_pack_kernel,
        grid=(n // tm,),
        in_specs=[pl.BlockSpec((tm, d), lambda i: (i, 0)), pl.BlockSpec((1, d), lambda i: (0, 0))],
        out_specs=[pl.BlockSpec((tm, d), lambda i: (i, 0)), pl.BlockSpec((tm, d // 2), lambda i: (i, 0))],
        out_shape=[jax.ShapeDtypeStruct((n, d), BF16), jax.ShapeDtypeStruct((n, d // 2), I32)],
        compiler_params=_cparams("parallel"),
        name="rmsnorm_pack",
    )(x, g.reshape(1, d))


def _unpack_bf16(p):
    lo = pltpu.bitcast(lax.shift_left(p, 16), F32).astype(BF16)
    hi = pltpu.bitcast(p & HI16, F32).astype(BF16)
    return lo, hi


def moe_plan(route, tm, m_pad):
    n = route.shape[0]
    ef = route[:, ROUTE_E1:ROUTE_E2 + 1].astype(I32).reshape(2 * n)
    onehot = (ef[:, None] == jnp.arange(N_EXPERTS, dtype=I32)[None, :]).astype(I32)
    counts = jnp.sum(onehot, axis=0)
    rank = jnp.sum((jnp.cumsum(onehot, axis=0) - onehot) * onehot, axis=1)
    padded = ((counts + tm - 1) // tm) * tm
    ends = jnp.cumsum(padded)
    dest = jnp.sum(onehot * (ends - padded)[None, :], axis=1) + rank
    tile_start = jnp.arange(m_pad // tm, dtype=I32) * tm
    tile_expert = jnp.minimum(jnp.sum((tile_start[:, None] >= ends[None, :]).astype(I32), axis=1), N_EXPERTS - 1)
    n_valid = (ends[-1] // tm).reshape(1)
    return dest, tile_expert, n_valid


def _dispatch_kernel(dest_ref, x_ref, xs_in_ref, xs_ref, sem):
    del xs_in_ref
    rows = x_ref.shape[0]

    def row_copy(r, s):
        d = dest_ref[0, 0, TOP_K * r + s]
        return pltpu.make_async_copy(x_ref.at[pl.ds(r, 1), :], xs_ref.at[pl.ds(d, 1), :], sem)

    def start(r, c):
        for s in range(TOP_K):
            row_copy(r, s).start()
        return c

    def wait(r, c):
        for s in range(TOP_K):
            row_copy(r, s).wait()
        return c

    lax.fori_loop(0, rows, start, 0)
    lax.fori_loop(0, rows, wait, 0)


def moe_dispatch(xp, dest, m_pad, tr):
    n, w = xp.shape
    return pl.pallas_call(
        _dispatch_kernel,
        grid=(n // tr,),
        in_specs=[
            pl.BlockSpec((1, 1, TOP_K * tr), lambda i: (i, 0, 0), memory_space=pltpu.SMEM),
            pl.BlockSpec((tr, w), lambda i: (i, 0)),
            pl.BlockSpec(memory_space=pl.ANY),
        ],
        out_specs=pl.BlockSpec(memory_space=pl.ANY),
        out_shape=jax.ShapeDtypeStruct((m_pad, w), I32),
        scratch_shapes=[pltpu.SemaphoreType.DMA(())],
        input_output_aliases={2: 0},
        compiler_params=_cparams("arbitrary"),
        name="moe_dispatch",
    )(dest.reshape(n // tr, 1, TOP_K * tr), xp, jnp.zeros((m_pad, w), I32))


def _moe_up_kernel(te_ref, nv_ref, x_ref, wg_ref, wu_ref, o_ref):
    del te_ref
    valid = pl.program_id(1) < nv_ref[0]

    @pl.when(valid)
    def _():
        lo, hi = _unpack_bf16(x_ref[...])
        half = x_ref.shape[1]
        a = _dot(lo, wg_ref[:half, :].astype(BF16)) + _dot(hi, wg_ref[half:, :].astype(BF16))
        b = _dot(lo, wu_ref[:half, :].astype(BF16)) + _dot(hi, wu_ref[half:, :].astype(BF16))
        o_ref[...] = (a * jax.nn.sigmoid(a) * b).astype(BF16)

    @pl.when(jnp.logical_not(valid))
    def _():
        o_ref[...] = jnp.zeros_like(o_ref)


def moe_up(xs, w_gate, w_up, tile_expert, n_valid, tm, tn):
    m = xs.shape[0]
    _, k, f = w_gate.shape
    grid_spec = pltpu.PrefetchScalarGridSpec(
        num_scalar_prefetch=2,
        grid=(f // tn, m // tm),
        in_specs=[
            pl.BlockSpec((tm, k // 2), lambda j, i, te, nv: (i, 0)),
            pl.BlockSpec((None, k, tn), lambda j, i, te, nv: (te[i], 0, j)),
            pl.BlockSpec((None, k, tn), lambda j, i, te, nv: (te[i], 0, j)),
        ],
        out_specs=pl.BlockSpec((tm, tn), lambda j, i, te, nv: (i, j)),
    )
    return pl.pallas_call(
        _moe_up_kernel,
        grid_spec=grid_spec,
        out_shape=jax.ShapeDtypeStruct((m, f), BF16),
        compiler_params=_cparams("parallel", "arbitrary"),
        name="moe_up",
    )(tile_expert, n_valid, xs, w_gate, w_up)


def _moe_down_kernel(te_ref, nv_ref, a_ref, w_ref, o_ref):
    del te_ref
    valid = pl.program_id(1) < nv_ref[0]

    @pl.when(valid)
    def _():
        o_ref[...] = _dot(a_ref[...], w_ref[...].astype(BF16))

    @pl.when(jnp.logical_not(valid))
    def _():
        o_ref[...] = jnp.zeros_like(o_ref)


def moe_down(a, w_down, tile_expert, n_valid, tm, tn):
    m, f = a.shape
    d = w_down.shape[2]
    grid_spec = pltpu.PrefetchScalarGridSpec(
        num_scalar_prefetch=2,
        grid=(d // tn, m // tm),
        in_specs=[
            pl.BlockSpec((tm, f), lambda j, i, te, nv: (i, 0)),
            pl.BlockSpec((None, f, tn), lambda j, i, te, nv: (te[i], 0, j)),
        ],
        out_specs=pl.BlockSpec((tm, tn), lambda j, i, te, nv: (i, j)),
    )
    return pl.pallas_call(
        _moe_down_kernel,
        grid_spec=grid_spec,
        out_shape=jax.ShapeDtypeStruct((m, d), F32),
        compiler_params=_cparams("parallel", "arbitrary"),
        name="moe_down",
    )(tile_expert, n_valid, a, w_down)


def _combine_kernel(dest_ref, y_ref, h_ref, rt_ref, o_ref, buf_ref, sem):
    rows = h_ref.shape[0]

    def row_copy(r, s):
        d = dest_ref[0, 0, TOP_K * r + s]
        return pltpu.make_async_copy(y_ref.at[pl.ds(d, 1), :], buf_ref.at[s, pl.ds(r, 1), :], sem.at[s])

    def start(r, c):
        for s in range(TOP_K):
            row_copy(r, s).start()
        return c

    def wait(r, c):
        for s in range(TOP_K):
            row_copy(r, s).wait()
        return c

    lax.fori_loop(0, rows, start, 0)
    lax.fori_loop(0, rows, wait, 0)
    rt = rt_ref[...]
    o_ref[...] = h_ref[...] + (rt[:, ROUTE_W1:ROUTE_W1 + 1] * buf_ref[0] + rt[:, ROUTE_W2:ROUTE_W2 + 1] * buf_ref[1])


def moe_combine(y, dest, route, h, tr):
    n, d = h.shape
    return pl.pallas_call(
        _combine_kernel,
        grid=(n // tr,),
        in_specs=[
            pl.BlockSpec((1, 1, TOP_K * tr), lambda i: (i, 0, 0), memory_space=pltpu.SMEM),
            pl.BlockSpec(memory_space=pl.ANY),
            pl.BlockSpec((tr, d), lambda i: (i, 0)),
            pl.BlockSpec((tr, LANES), lambda i: (i, 0)),
        ],
        out_specs=pl.BlockSpec((tr, d), lambda i: (i, 0)),
        out_shape=jax.ShapeDtypeStruct((n, d), F32),
        scratch_shapes=[pltpu.VMEM((TOP_K, tr, d), F32), pltpu.SemaphoreType.DMA((TOP_K,))],
        compiler_params=_cparams("arbitrary"),
        name="moe_combine",
    )(dest.reshape(n // tr, 1, TOP_K * tr), y, h, route)


def _prep_kernel(q_ref, k_ref, v_ref, qi_ref, sm_ref, c128_ref, s128_ref, c64_ref, s64_ref,
                 gq_ref, gk_ref, gki_ref,
                 qo_ref, kf_ref, kb_ref, vb_ref, qio_ref, kif_ref, kib_ref):
    c128, s128 = c128_ref[...], s128_ref[...]
    c64, s64 = c64_ref[...], s64_ref[...]
    lane = lax.broadcasted_iota(I32, c64.shape, 1)
    lane_lo = (lane & 63) < 32

    def rope128(x):
        return x * c128 + pltpu.roll(x, 64, 1) * s128

    def rope64(x):
        partner = jnp.where(lane_lo, pltpu.roll(x, 96, 1), pltpu.roll(x, 32, 1))
        return x * c64 + partner * s64

    def norm128(x, g):
        ms = jnp.mean(x * x, axis=-1, keepdims=True)
        return x * lax.rsqrt(ms + EPS) * g

    gq, gk = gq_ref[...], gk_ref[...]
    for h in range(DSA_HEADS):
        sl = slice(h * HEAD_DIM, (h + 1) * HEAD_DIM)
        qo_ref[:, sl] = (rope128(norm128(q_ref[:, sl], gq)) * Q_SCALE).astype(BF16)
    for h in range(DSA_KV):
        sl = slice(h * HEAD_DIM, (h + 1) * HEAD_DIM)
        kr = rope128(norm128(k_ref[:, sl], gk))
        kf_ref[:, sl] = kr
        kb_ref[:, sl] = kr.astype(BF16)
    ones = jnp.ones((v_ref.shape[0], HEAD_DIM), BF16)
    for h in range(DSA_KV):
        vb_ref[:, 2 * h * HEAD_DIM:(2 * h + 1) * HEAD_DIM] = v_ref[:, h * HEAD_DIM:(h + 1) * HEAD_DIM].astype(BF16)
        vb_ref[:, (2 * h + 1) * HEAD_DIM:(2 * h + 2) * HEAD_DIM] = ones
    for t in range(IDX_HEADS * IDX_DIM // LANES):
        sl = slice(t * LANES, (t + 1) * LANES)
        qio_ref[:, sl] = rope64(qi_ref[:, sl]).astype(BF16)
    sm = sm_ref[...]
    ms = jnp.sum(jnp.where(lane < IDX_DIM, sm * sm, 0.0), axis=-1, keepdims=True) * (1.0 / IDX_DIM)
    ki = rope64(sm * lax.rsqrt(ms + EPS) * gki_ref[...])
    kif_ref[...] = ki[:, :IDX_DIM]
    kib_ref[...] = ki[:, :IDX_DIM].astype(BF16)


def dsa_prep(proj, tabs, g_q, g_k, g_ki, tm):
    n = proj.shape[0]
    c128, s128, c64, s64 = tabs
    gki_pad = jnp.pad(g_ki, (0, LANES - IDX_DIM)).reshape(1, LANES)
    row = lambda w, off: pl.BlockSpec((tm, w), lambda i: (i, off // w))
    tab = pl.BlockSpec((tm, LANES), lambda i: (i, 0))
    vec = pl.BlockSpec((1, LANES), lambda i: (0, 0))
    out = lambda w: pl.BlockSpec((tm, w), lambda i: (i, 0))
    kvw = DSA_KV * HEAD_DIM
    return pl.pallas_call(
        _prep_kernel,
        grid=(n // tm,),
        in_specs=[row(MIX_B, OFF_DSA_Q), row(kvw, OFF_DSA_K), row(kvw, OFF_DSA_V),
                  row(IDX_HEADS * IDX_DIM, OFF_IDX_Q), row(LANES, OFF_SMALL),
                  tab, tab, tab, tab, vec, vec, vec],
        out_specs=[out(MIX_B), out(kvw), out(kvw), out(2 * kvw), out(IDX_HEADS * IDX_DIM),
                   out(IDX_DIM), out(IDX_DIM)],
        out_shape=[
            jax.ShapeDtypeStruct((n, MIX_B), BF16),
            jax.ShapeDtypeStruct((n, kvw), F32),
            jax.ShapeDtypeStruct((n, kvw), BF16),
            jax.ShapeDtypeStruct((n, 2 * kvw), BF16),
            jax.ShapeDtypeStruct((n, IDX_HEADS * IDX_DIM), BF16),
            jax.ShapeDtypeStruct((n, IDX_DIM), F32),
            jax.ShapeDtypeStruct((n, IDX_DIM), BF16),
        ],
        compiler_params=_cparams("parallel"),
        name="dsa_prep",
    )(proj, proj, proj, proj, proj, c128, s128, c64, s64,
      g_q.reshape(1, LANES), g_k.reshape(1, LANES), gki_pad)


def rope_tables(pos):
    def tab(half, reps):
        inv = ROPE_THETA ** (-jnp.arange(half, dtype=F32) / half)
        ang = pos.astype(F32)[:, None] * inv[None, :]
        cos, sin = jnp.cos(ang), jnp.sin(ang)
        return jnp.tile(jnp.concatenate([cos, cos], -1), (1, reps)), jnp.tile(jnp.concatenate([-sin, sin], -1), (1, reps))
    c128, s128 = tab(HEAD_DIM // 2, 1)
    c64, s64 = tab(IDX_DIM // 2, 2)
    return c128, s128, c64, s64


def _gla_kernel(q_ref, k_ref, v_ref, r_ref, sm_ref, wg2_ref, bg_ref, gout_ref, s0_ref,
                o_ref, sout_ref, st_ref, *, chunk, n_chunks):
    t = pl.program_id(1)

    @pl.when(t == 0)
    def _():
        for h in range(GLA_HEADS):
            st_ref[h] = s0_ref[h].T

    row = lax.broadcasted_iota(I32, (chunk, chunk), 0)
    col = lax.broadcasted_iota(I32, (chunk, chunk), 1)
    causal = row >= col
    tril = jnp.where(causal, 1.0, 0.0).astype(BF16)
    gout = gout_ref[...]

    heads = range(GLA_HEADS)
    dks = [slice(h * GLA_DK, (h + 1) * GLA_DK) for h in heads]
    dvs = [slice(h * GLA_DV, (h + 1) * GLA_DV) for h in heads]
    ws = [_split3(wg2_ref[h]) for h in heads]
    bgs = [bg_ref[:, dks[h]] for h in heads]
    sts = [st_ref[h] for h in heads]
    for c in range(n_chunks):
        sl = pl.ds(c * chunk, chunk)
        s1, s2, s3 = _split3(sm_ref[sl, :])
        qs = [q_ref[sl, dks[h]] * (GLA_DK ** -0.5) for h in heads]
        ks = [k_ref[sl, dks[h]] for h in heads]
        vs = [v_ref[sl, dvs[h]].astype(BF16) for h in heads]
        rs = [r_ref[sl, dvs[h]] for h in heads]
        zs = [(_dot(s1, w[0]) + (_dot(s1, w[1]) + _dot(s2, w[0]))
               + (_dot(s1, w[2]) + _dot(s2, w[1]) + _dot(s3, w[0]))) + bgs[h] for h, w in enumerate(ws)]
        gs = [_split3((jnp.minimum(z, 0.0) - jnp.log1p(jnp.exp(-jnp.abs(z)))) * (1.0 / GLA_TAU)) for z in zs]
        bs = [_dot(tril, g[0]) + _dot(tril, g[1]) + _dot(tril, g[2]) for g in gs]
        b_last = [b[chunk - 1:chunk, :] for b in bs]
        qe = [(qs[h] * jnp.exp(bs[h])).astype(BF16) for h in heads]
        ke = [(ks[h] * jnp.exp(-bs[h])).astype(BF16) for h in heads]
        kd = [(ks[h] * jnp.exp(b_last[h] - bs[h])).astype(BF16) for h in heads]
        a = [jnp.where(causal, _dot_nt(qe[h], ke[h]), 0.0).astype(BF16) for h in heads]
        o = [_dot(a[h], vs[h]) + _dot_nt(qe[h], sts[h].astype(BF16)) for h in heads]
        sts = [jnp.exp(b_last[h]) * sts[h] + _dot_tn(vs[h], kd[h]) for h in heads]
        for h in heads:
            ms = jnp.mean(o[h] * o[h], axis=-1, keepdims=True)
            on = o[h] * lax.rsqrt(ms + EPS) * gout
            o_ref[sl, dvs[h]] = (rs[h] * jax.nn.sigmoid(rs[h]) * on).astype(BF16)
    for h in heads:
        st_ref[h] = sts[h]

    @pl.when(t == pl.num_programs(1) - 1)
    def _():
        for h in range(GLA_HEADS):
            sout_ref[h] = st_ref[h].T


def gla_scan(proj, wg2_pad, b_g, g_out, s0, row0, seq, chunk, tc):
    nb = s0.shape[0]
    nt = seq // tc
    rb0 = row0 // tc
    hk, hv = GLA_HEADS * GLA_DK, GLA_HEADS * GLA_DV
    rows = lambda w, off: pl.BlockSpec((tc, w), lambda b, t: (rb0 + b * nt + t, off // w))
    state = pl.BlockSpec((None, GLA_HEADS, GLA_DK, GLA_DV), lambda b, t: (b, 0, 0, 0))
    kern = functools.partial(_gla_kernel, chunk=chunk, n_chunks=tc // chunk)
    return pl.pallas_call(
        kern,
        grid=(nb, nt),
        in_specs=[
            rows(hk, OFF_GLA_Q), rows(hk, OFF_GLA_K), rows(hv, OFF_GLA_V), rows(hv, OFF_GLA_R),
            rows(LANES, OFF_SMALL),
            pl.BlockSpec((GLA_HEADS, LANES, GLA_DK), lambda b, t: (0, 0, 0)),
            pl.BlockSpec((1, hk), lambda b, t: (0, 0)),
            pl.BlockSpec((1, GLA_DV), lambda b, t: (0, 0)),
            state,
        ],
        out_specs=[pl.BlockSpec((tc, hv), lambda b, t: (b * nt + t, 0)), state],
        out_shape=[
            jax.ShapeDtypeStruct((nb * seq, MIX_A), BF16),
            jax.ShapeDtypeStruct((nb, GLA_HEADS, GLA_DK, GLA_DV), F32),
        ],
        scratch_shapes=[pltpu.VMEM((GLA_HEADS, GLA_DV, GLA_DK), F32)],
        compiler_params=_cparams("parallel", "arbitrary"),
        name="gla_scan",
    )(proj, proj, proj, proj, proj, wg2_pad, b_g.reshape(1, -1), g_out.reshape(1, -1), s0)


def gla_gate_weight(w_g2):
    w = w_g2.reshape(GLA_RANK, GLA_HEADS, GLA_DK).transpose(1, 0, 2)
    return jnp.pad(w, ((0, 0), (SM_GLR, LANES - SM_GLR - GLA_RANK), (0, 0)))


def _dsa_kernel(q_ref, qi_ref, sm_ref, k_ref, v_ref, ki_ref, o_ref,
                key_scr, half_scr, vec_scr, vec16_scr, qis_scr, qg_scr, m_scr, a_scr, p_scr, acc_scr,
                *, qb, tk, lp, n_sel, l_valid, causal_blocks, qpos_static):
    gq = DSA_GROUP * qb
    if causal_blocks:
        i = pl.program_id(1)
        qpos0 = i * qb
        nt = ((i + 1) * qb + tk - 1) // tk
    else:
        qpos0 = qpos_static
        nt = lp // tk
    nsub = tk // LANES
    idx_bits = max(1, (lp - 1).bit_length())

    for h in range(IDX_HEADS):
        qis_scr[h * qb:(h + 1) * qb, :] = qi_ref[:, h * IDX_DIM:(h + 1) * IDX_DIM]
    for g in range(DSA_KV):
        for hh in range(DSA_GROUP):
            hd = g * DSA_GROUP + hh
            qg_scr[g, hh * qb:(hh + 1) * qb, :] = q_ref[:, hd * HEAD_DIM:(hd + 1) * HEAD_DIM]
    wq = sm_ref[:, SM_IDX_W:SM_IDX_W + IDX_HEADS] * ((IDX_DIM ** -0.5) * (IDX_HEADS ** -0.5))
    w_cols = [wq[:, h:h + 1] for h in range(IDX_HEADS)]
    row_chunk = (qpos0 + lax.broadcasted_iota(I32, (qb, tk), 0)) >> 6
    lane_t = lax.broadcasted_iota(I32, (qb, tk), 1)

    def p1(j, carry):
        ks = pl.multiple_of(j * tk, tk)
        s_all = _dot_nt(qis_scr[...], ki_ref[pl.ds(ks, tk), :])
        acc = jnp.zeros((qb, tk), F32)
        for h in range(IDX_HEADS):
            acc = acc + jnp.maximum(s_all[h * qb:(h + 1) * qb, :], 0.0) * w_cols[h]
        kpos = ks + lane_t
        adm = jnp.where((kpos >> 6) <= row_chunk, kpos, l_valid) < l_valid
        score = jnp.where(adm, acc, -jnp.inf)
        bits = pltpu.bitcast(score, I32)
        key = bits ^ ((bits >> 31) & 0x7FFFFFFF)
        key_scr[:, pl.ds(ks, tk)] = key
        half_scr[:, pl.ds(ks, tk)] = (key >> 16).astype(I16)
        return carry

    lax.fori_loop(0, nt, p1, 0)

    rh = min(qb, 128)

    row_groups = list(range(0, qb, rh))

    def count(pred, *row_vecs):
        for i, v in enumerate(row_vecs):
            vec_scr[i] = jnp.broadcast_to(v, (qb, LANES))

        def body(j, accs):
            accs = list(accs)
            for gi, r0 in enumerate(row_groups):
                vecs = [vec_scr[i, r0:r0 + rh, :] for i in range(len(row_vecs))]
                for c in range(nsub):
                    off = pl.multiple_of(j * tk + c * LANES, LANES)
                    blk = key_scr[r0:r0 + rh, pl.ds(off, LANES)]
                    idx = off + lax.broadcasted_iota(I32, (rh, LANES), 1)
                    accs[gi] = accs[gi] + jnp.where(pred(blk, idx, *vecs), 1, 0)
            return tuple(accs)

        accs = lax.fori_loop(0, nt, body, tuple(jnp.zeros((rh, LANES), I32) for _ in row_groups))
        acc = accs[0] if len(accs) == 1 else jnp.concatenate(accs, axis=0)
        return jnp.sum(acc, axis=1, keepdims=True)

    def count16(vec, strict):
        vec16_scr[...] = jnp.broadcast_to(vec, (qb, LANES)).astype(I16)
        one, zero = jnp.ones((rh, LANES), I16), jnp.zeros((rh, LANES), I16)

        def body(j, accs):
            accs = list(accs)
            for gi, r0 in enumerate(row_groups):
                c16 = vec16_scr[r0:r0 + rh, :]
                for c in range(nsub):
                    off = pl.multiple_of(j * tk + c * LANES, LANES)
                    blk = half_scr[r0:r0 + rh, pl.ds(off, LANES)]
                    accs[gi] = accs[gi] + jnp.where((blk > c16) if strict else (blk >= c16), one, zero)
            return tuple(accs)

        accs = lax.fori_loop(0, nt, body, tuple(zero for _ in row_groups))
        acc = accs[0] if len(accs) == 1 else jnp.concatenate(accs, axis=0)
        return jnp.sum(acc.astype(I32), axis=1, keepdims=True)

    def search16(need):
        c0 = count16(jnp.zeros((qb, 1), I32), False)
        t = jnp.where(c0 >= need, 0, HALF_MIN).astype(I32)

        def bit_step(it, t):
            cand = t + lax.shift_left(jnp.int32(1), 14 - it)
            return jnp.where(count16(cand, False) >= need, cand, t)

        return lax.fori_loop(0, 15, bit_step, t)

    thr_hi = search16(n_sel)
    need_lo = n_sel - count16(thr_hi, True)

    def lower_halves(j, carry):
        ks = pl.multiple_of(j * tk, tk)
        key = key_scr[:, pl.ds(ks, tk)]
        lo = (key & 0xFFFF) + HALF_MIN
        half_scr[:, pl.ds(ks, tk)] = jnp.where((key >> 16) == thr_hi, lo, HALF_MIN).astype(I16)
        return carry

    lax.fori_loop(0, nt, lower_halves, 0)
    thr = lax.shift_left(thr_hi, 16) | (search16(need_lo) - HALF_MIN)
    cnt_ge = count(lambda blk, idx, t: blk >= t, thr)
    tie_rows = jnp.where(thr > KEY_NEG_INF, cnt_ge, 0) > n_sel
    any_tie = jnp.max(jnp.where(tie_rows, 1, 0)) > 0

    def tie_cut():
        need = n_sel - count(lambda blk, idx, t: blk > t, thr)
        pos = jnp.zeros((qb, 1), I32)
        for b in range(idx_bits - 1, -1, -1):
            cand = pos + (1 << b)
            f = count(lambda blk, idx, t, c: jnp.where(blk == t, idx, lp) < c, thr, cand)
            pos = jnp.where(f < need, cand, pos)
        return pos

    cut = lax.cond(any_tie, tie_cut, lambda: jnp.full((qb, 1), lp, I32))

    m_scr[...] = jnp.full(m_scr.shape, M_INIT, F32)
    acc_scr[...] = jnp.zeros(acc_scr.shape, F32)
    rb = min(qb, 32)

    def p3(j, carry):
        ks = pl.multiple_of(j * tk, tk)
        key = key_scr[:, pl.ds(ks, tk)]
        idx = ks + lane_t
        eq_ok = jnp.where(key == thr, idx, lp + 1) <= cut
        bias = jnp.where(key > thr, 0.0, jnp.where(eq_ok, 0.0, MASK_BIAS))
        bias = jnp.where(key > KEY_NEG_INF, bias, MASK_BIAS)
        s_all = [_dot_nt(qg_scr[g], k_ref[pl.ds(ks, tk), g * HEAD_DIM:(g + 1) * HEAD_DIM])
                 for g in range(DSA_KV)]
        for g in range(DSA_KV):
            for r0 in range(0, gq, rb):
                rs = slice(r0, r0 + rb)
                s = s_all[g][rs, :] + bias[r0 % qb:r0 % qb + rb, :]
                m_old = m_scr[g, rs, :]
                m_new = jnp.maximum(m_old, jnp.max(s, axis=1, keepdims=True))
                p_scr[g, rs, :] = jnp.exp2(s - m_new).astype(BF16)
                a_scr[g, rs, :] = jnp.exp2(m_old - m_new)
                m_scr[g, rs, :] = m_new
        for g in range(DSA_KV):
            vt = v_ref[pl.ds(ks, tk), 2 * g * HEAD_DIM:(2 * g + 2) * HEAD_DIM]
            acc_scr[g] = a_scr[g] * acc_scr[g] + _dot(p_scr[g], vt)
        return carry

    lax.fori_loop(0, nt, p3, 0)

    for g in range(DSA_KV):
        acc = acc_scr[g]
        o = acc[:, :HEAD_DIM] / acc[:, HEAD_DIM:]
        for hh in range(DSA_GROUP):
            hd = g * DSA_GROUP + hh
            o_ref[:, hd * HEAD_DIM:(hd + 1) * HEAD_DIM] = o[hh * qb:(hh + 1) * qb, :].astype(BF16)


def dsa_attention(q, qi, proj, k, v, ki, *, row0, nq, qb, tk, n_sel, l_valid, causal_blocks, qpos_static):
    nb, lp, kvw = k.shape
    rb0 = row0 // qb
    gq = DSA_GROUP * qb
    kern = functools.partial(_dsa_kernel, qb=qb, tk=tk, lp=lp, n_sel=n_sel, l_valid=l_valid,
                             causal_blocks=causal_blocks, qpos_static=qpos_static)
    return pl.pallas_call(
        kern,
        grid=(nb, nq),
        in_specs=[
            pl.BlockSpec((qb, MIX_B), lambda b, i: (rb0 + b * nq + i, 0)),
            pl.BlockSpec((qb, IDX_HEADS * IDX_DIM), lambda b, i: (rb0 + b * nq + i, 0)),
            pl.BlockSpec((qb, LANES), lambda b, i: (rb0 + b * nq + i, OFF_SMALL // LANES)),
            pl.BlockSpec((None, lp, kvw), lambda b, i: (b, 0, 0)),
            pl.BlockSpec((None, lp, 2 * kvw), lambda b, i: (b, 0, 0)),
            pl.BlockSpec((None, lp, IDX_DIM), lambda b, i: (b, 0, 0)),
        ],
        out_specs=pl.BlockSpec((qb, MIX_B), lambda b, i: (b * nq + i, 0)),
        out_shape=jax.ShapeDtypeStruct((nb * nq * qb, MIX_B), BF16),
        scratch_shapes=[
            pltpu.VMEM((qb, lp), I32),
            pltpu.VMEM((qb, lp), I16),
            pltpu.VMEM((2, qb, LANES), I32),
            pltpu.VMEM((qb, LANES), I16),
            pltpu.VMEM((IDX_HEADS * qb, IDX_DIM), BF16),
            pltpu.VMEM((DSA_KV, gq, HEAD_DIM), BF16),
            pltpu.VMEM((DSA_KV, gq, 1), F32),
            pltpu.VMEM((DSA_KV, gq, 1), F32),
            pltpu.VMEM((DSA_KV, gq, tk), BF16),
            pltpu.VMEM((DSA_KV, gq, 2 * HEAD_DIM), F32),
        ],
        compiler_params=_cparams("parallel", "arbitrary"),
        name="dsa_attention",
    )(q, qi, proj, k, v, ki)


def permute_w_in(w_in):
    offs, o = {}, 0
    for name, size in COL_SIZES:
        offs[name] = (o, size)
        o += size
    parts = [w_in[:, offs[n][0]:offs[n][0] + offs[n][1]] for n in PERM_ORDER]
    used = sum(offs[n][1] for n in PERM_ORDER)
    parts.append(jnp.zeros((w_in.shape[0], P_COLS - used), w_in.dtype))
    return jnp.concatenate(parts, axis=1)


TM_BIG = 1536
TM_DOWN = 768
TC_GLA = 256
TK_DSA = 512
QB_DSA = 256
SAMPLE_LP = 2176
TM_MOE = 512
TR_MOE = 768
MOE_ROWS = -(-(TOP_K * N_TOK + N_EXPERTS * (TM_MOE - 1)) // TM_MOE) * TM_MOE


def kernel(x_prompt, x_sample, cache_k, cache_v, cache_ki, state_gla, g_attn, w_in, w_gla_gate2, b_gla_gate,
           g_gla_out, g_q, g_k, g_ki, w_branch_a, w_branch_b, w_out, g_ffn, w_ff_gate, w_ff_up, w_ff_down,
           w_router, w_moe_gate, w_moe_up, w_moe_down):
    h = jnp.concatenate([x_prompt.reshape(N_PROMPT, D_MODEL), x_sample.reshape(N_SAMPLE, D_MODEL)], axis=0)
    pos = jnp.concatenate([jnp.tile(jnp.arange(SEQ, dtype=I32), BATCH),
                           jnp.tile(PAST_LEN + jnp.arange(DEC_SEQ, dtype=I32), DEC_BATCH)])
    tabs = rope_tables(pos)
    kvw = DSA_KV * HEAD_DIM
    l_sample = PAST_LEN + DEC_SEQ
    zeros_state = jnp.zeros((BATCH, GLA_HEADS, GLA_DK, GLA_DV), F32)
    outs = {n: [] for n in ('kp', 'vp', 'kip', 'sp', 'ks', 'vs', 'kis', 'ss')}

    for l in range(DEPTH):
        hn = rmsnorm_cast(h, g_attn[l], TM_DOWN)
        proj = matmul(hn, permute_w_in(w_in[l]), TM_BIG, 512, name="in_proj")
        q_b, k_f, k_b, v_b, qi_b, ki_f, ki_b = dsa_prep(proj, tabs, g_q[l], g_k[l], g_ki[l], 512)

        wg2 = gla_gate_weight(w_gla_gate2[l])
        oa_p, s_p = gla_scan(proj, wg2, b_gla_gate[l], g_gla_out[l], zeros_state, 0, SEQ, CHUNK, TC_GLA)
        oa_s, s_s = gla_scan(proj, wg2, b_gla_gate[l], g_gla_out[l], state_gla[l], N_PROMPT, DEC_SEQ,
                             DEC_SEQ, DEC_SEQ)

        ob_p = dsa_attention(
            q_b, qi_b, proj,
            k_b[:N_PROMPT].reshape(BATCH, SEQ, kvw), v_b[:N_PROMPT].reshape(BATCH, SEQ, 2 * kvw),
            ki_b[:N_PROMPT].reshape(BATCH, SEQ, IDX_DIM),
            row0=0, nq=SEQ // QB_DSA, qb=QB_DSA, tk=TK_DSA, n_sel=min(TOPK_MAX, SEQ // 4), l_valid=SEQ,
            causal_blocks=True, qpos_static=0)

        def with_cache(cache_parts, new, width):
            c = jnp.concatenate([p.astype(BF16) for p in cache_parts], axis=-1)
            n = new[N_PROMPT:].reshape(DEC_BATCH, DEC_SEQ, width)
            pad = jnp.zeros((DEC_BATCH, SAMPLE_LP - l_sample, width), BF16)
            return jnp.concatenate([c, n, pad], axis=1)

        ck, cv = cache_k[l], cache_v[l]
        ones = jnp.ones((DEC_BATCH, PAST_LEN, HEAD_DIM), BF16)
        ob_s = dsa_attention(
            q_b, qi_b, proj,
            with_cache([ck[:, :, g, :] for g in range(DSA_KV)], k_b, kvw),
            with_cache([p for g in range(DSA_KV) for p in (cv[:, :, g, :], ones)], v_b, 2 * kvw),
            with_cache([cache_ki[l]], ki_b, IDX_DIM),
            row0=N_PROMPT, nq=1, qb=DEC_SEQ, tk=SAMPLE_LP, n_sel=min(TOPK_MAX, l_sample // 4),
            l_valid=l_sample, causal_blocks=False, qpos_static=PAST_LEN)

        o_a = jnp.concatenate([oa_p, oa_s], axis=0)
        o_b = jnp.concatenate([ob_p, ob_s], axis=0)
        merged = merge_branches(o_a, o_b, w_branch_a[l], w_branch_b[l], proj, TM_BIG, 512)
        h = matmul(merged, w_out[l], TM_BIG, 512, residual=h, name="out_proj")

        j = l // 2
        if l % 2 == 0:
            hn = rmsnorm_cast(h, g_ffn[l], TM_DOWN)
            a = swiglu_up(hn, w_ff_gate[j], w_ff_up[j], TM_BIG, 256)
            h = matmul(a, w_ff_down[j], TM_DOWN, 512, residual=h, name="ffn_down", weight_resident=True)
        else:
            hn, hn_packed = rmsnorm_pack(h, g_ffn[l], TM_DOWN)
            route = moe_router(hn, w_router[j], TM_DOWN)
            dest, tile_expert, n_valid = moe_plan(route, TM_MOE, MOE_ROWS)
            xs = moe_dispatch(hn_packed, dest, MOE_ROWS, TR_MOE)
            a = moe_up(xs, w_moe_gate[j], w_moe_up[j], tile_expert, n_valid, TM_MOE, 512)
            y = moe_down(a, w_moe_down[j], tile_expert, n_valid, TM_MOE, 512)
            h = moe_combine(y, dest, route, h, TR_MOE)

        v_f = proj[:, OFF_DSA_V:OFF_DSA_V + kvw]
        outs['kp'].append(k_f[:N_PROMPT].reshape(BATCH, SEQ, DSA_KV, HEAD_DIM))
        outs['vp'].append(v_f[:N_PROMPT].reshape(BATCH, SEQ, DSA_KV, HEAD_DIM))
        outs['kip'].append(ki_f[:N_PROMPT].reshape(BATCH, SEQ, IDX_DIM))
        outs['sp'].append(s_p)
        outs['ks'].append(k_f[N_PROMPT:].reshape(DEC_BATCH, DEC_SEQ, DSA_KV, HEAD_DIM))
        outs['vs'].append(v_f[N_PROMPT:].reshape(DEC_BATCH, DEC_SEQ, DSA_KV, HEAD_DIM))
        outs['kis'].append(ki_f[N_PROMPT:].reshape(DEC_BATCH, DEC_SEQ, IDX_DIM))
        outs['ss'].append(s_s)

    st = {n: jnp.stack(v) for n, v in outs.items()}
    return (h[:N_PROMPT].reshape(BATCH, SEQ, D_MODEL), h[N_PROMPT:].reshape(DEC_BATCH, DEC_SEQ, D_MODEL),
            st['kp'], st['vp'], st['kip'], st['sp'], st['ks'], st['vs'], st['kis'], st['ss'])
```

```python
import functools

import jax
import jax.numpy as jnp
from jax import lax
from jax.experimental import pallas as pl
from jax.experimental.pallas import tpu as pltpu

F32 = jnp.float32
BF16 = jnp.bfloat16
I32 = jnp.int32

D_MODEL = 2048
BATCH = 2
SEQ = 8192
DEPTH = 2
DEC_BATCH = 32
DEC_SEQ = 16
PAST_LEN = 2048
CHUNK = 64
QBLK = 128
ROPE_THETA = 10000.0
EPS = 1e-6
GLA_HEADS = 4
GLA_DK = 128
GLA_DV = 256
GLA_RANK = 16
GLA_TAU = 16.0
DSA_HEADS = 8
DSA_KV = 2
DSA_GROUP = DSA_HEADS // DSA_KV
HEAD_DIM = 128
IDX_HEADS = 8
IDX_DIM = 64
TOPK_MAX = 256
MIX_A = GLA_HEADS * GLA_DV
MIX_B = DSA_HEADS * HEAD_DIM
D_FF = 5632
N_EXPERTS = 8
TOP_K = 2

N_PROMPT = BATCH * SEQ
N_SAMPLE = DEC_BATCH * DEC_SEQ
N_TOK = N_PROMPT + N_SAMPLE

LANES = 128
VMEM_LIMIT_BYTES = 56 * 1024 * 1024

COL_SIZES = (
    ('gla_q', 512), ('gla_k', 512), ('gla_v', 1024), ('gla_glr', 16), ('gla_r', 1024),
    ('dsa_q', 1024), ('dsa_k', 256), ('dsa_v', 256), ('idx_q', 512), ('idx_k', 64),
    ('idx_w', 8), ('gate_a', 2048), ('gate_b', 2048),
)
PERM_ORDER = ('gla_q', 'gla_k', 'gla_v', 'gla_r', 'dsa_q', 'dsa_k', 'dsa_v', 'idx_q',
              'gate_a', 'gate_b', 'idx_k', 'gla_glr', 'idx_w')
P_COLS = 9728
OFF_GLA_Q, OFF_GLA_K, OFF_GLA_V, OFF_GLA_R = 0, 512, 1024, 2048
OFF_DSA_Q, OFF_DSA_K, OFF_DSA_V, OFF_IDX_Q = 3072, 4096, 4352, 4608
OFF_GATE_A, OFF_GATE_B, OFF_SMALL = 5120, 7168, 9216
SM_IDX_K, SM_GLR, SM_IDX_W = 0, 64, 80

ROUTE_E1, ROUTE_E2, ROUTE_W1, ROUTE_W2 = 0, 1, 2, 3
HI16 = -65536
KEY_NEG_INF = -2139095041
INT_MIN = -2147483648
MASK_BIAS = -1e30
M_INIT = -5e29
Q_SCALE = (HEAD_DIM ** -0.5) * 1.4426950408889634


def _cparams(*sem):
    return pltpu.CompilerParams(dimension_semantics=sem, vmem_limit_bytes=VMEM_LIMIT_BYTES)


def _dot(a, b):
    return jnp.dot(a, b, preferred_element_type=F32)


def _dot_nt(a, b):
    return lax.dot_general(a, b, (((1,), (1,)), ((), ())), preferred_element_type=F32)


def _dot_tn(a, b):
    return lax.dot_general(a, b, (((0,), (0,)), ((), ())), preferred_element_type=F32)


def _split3(a):
    a1 = a.astype(BF16)
    r1 = a - a1.astype(F32)
    a2 = r1.astype(BF16)
    a3 = (r1 - a2.astype(F32)).astype(BF16)
    return a1, a2, a3


def _rmsnorm_kernel(x_ref, g_ref, o_ref):
    x = x_ref[...]
    ms = jnp.mean(x * x, axis=-1, keepdims=True)
    o_ref[...] = ((x * lax.rsqrt(ms + EPS)) * g_ref[...]).astype(BF16)


def rmsnorm_cast(x, g, tm):
    n, d = x.shape
    return pl.pallas_call(
        _rmsnorm_kernel,
        grid=(n // tm,),
        in_specs=[pl.BlockSpec((tm, d), lambda i: (i, 0)), pl.BlockSpec((1, d), lambda i: (0, 0))],
        out_specs=pl.BlockSpec((tm, d), lambda i: (i, 0)),
        out_shape=jax.ShapeDtypeStruct((n, d), BF16),
        compiler_params=_cparams("parallel"),
        name="rmsnorm_cast",
    )(x, g.reshape(1, d))


def _mm_kernel(x_ref, w_ref, o_ref):
    o_ref[...] = _dot(x_ref[...], w_ref[...].astype(BF16))


def _mm_res_kernel(x_ref, w_ref, r_ref, o_ref):
    o_ref[...] = r_ref[...] + _dot(x_ref[...], w_ref[...].astype(BF16))


def matmul(x, w, tm, tn, residual=None, name="matmul", weight_resident=False):
    m, k = x.shape
    n = w.shape[1]
    if weight_resident:
        grid = (n // tn, m // tm)
        rc = lambda a, b: (b, a)
    else:
        grid = (m // tm, n // tn)
        rc = lambda a, b: (a, b)
    in_specs = [pl.BlockSpec((tm, k), lambda a, b: (rc(a, b)[0], 0)),
                pl.BlockSpec((k, tn), lambda a, b: (0, rc(a, b)[1]))]
    args = [x, w]
    body = _mm_kernel
    if residual is not None:
        in_specs.append(pl.BlockSpec((tm, tn), lambda a, b: rc(a, b)))
        args.append(residual)
        body = _mm_res_kernel
    return pl.pallas_call(
        body,
        grid=grid,
        in_specs=in_specs,
        out_specs=pl.BlockSpec((tm, tn), lambda a, b: rc(a, b)),
        out_shape=jax.ShapeDtypeStruct((m, n), F32),
        compiler_params=_cparams("parallel", "parallel"),
        name=name,
    )(*args)


def _merge_kernel(oa_ref, ob_ref, wa_ref, wb_ref, ga_ref, gb_ref, o_ref):
    a = _dot(oa_ref[...], wa_ref[...].astype(BF16))
    b = _dot(ob_ref[...], wb_ref[...].astype(BF16))
    o_ref[...] = (jax.nn.sigmoid(ga_ref[...]) * a + jax.nn.sigmoid(gb_ref[...]) * b).astype(BF16)


def merge_branches(o_a, o_b, w_pa, w_pb, proj, tm, tn):
    m = o_a.shape[0]
    n = w_pa.shape[1]
    ja, jb = OFF_GATE_A // tn, OFF_GATE_B // tn
    return pl.pallas_call(
        _merge_kernel,
        grid=(m // tm, n // tn),
        in_specs=[
            pl.BlockSpec((tm, MIX_A), lambda i, j: (i, 0)),
            pl.BlockSpec((tm, MIX_B), lambda i, j: (i, 0)),
            pl.BlockSpec((MIX_A, tn), lambda i, j: (0, j)),
            pl.BlockSpec((MIX_B, tn), lambda i, j: (0, j)),
            pl.BlockSpec((tm, tn), lambda i, j: (i, ja + j)),
            pl.BlockSpec((tm, tn), lambda i, j: (i, jb + j)),
        ],
        out_specs=pl.BlockSpec((tm, tn), lambda i, j: (i, j)),
        out_shape=jax.ShapeDtypeStruct((m, n), BF16),
        compiler_params=_cparams("parallel", "parallel"),
        name="merge_branches",
    )(o_a, o_b, w_pa, w_pb, proj, proj)


def _swiglu_kernel(x_ref, wg_ref, wu_ref, o_ref):
    x = x_ref[...]
    a = _dot(x, wg_ref[...].astype(BF16))
    b = _dot(x, wu_ref[...].astype(BF16))
    o_ref[...] = (a * jax.nn.sigmoid(a) * b).astype(BF16)


def swiglu_up(x, w_gate, w_up, tm, tn):
    m, k = x.shape
    f = w_gate.shape[1]
    return pl.pallas_call(
        _swiglu_kernel,
        grid=(m // tm, f // tn),
        in_specs=[
            pl.BlockSpec((tm, k), lambda i, j: (i, 0)),
            pl.BlockSpec((k, tn), lambda i, j: (0, j)),
            pl.BlockSpec((k, tn), lambda i, j: (0, j)),
        ],
        out_specs=pl.BlockSpec((tm, tn), lambda i, j: (i, j)),
        out_shape=jax.ShapeDtypeStruct((m, f), BF16),
        compiler_params=_cparams("parallel", "parallel"),
        name="swiglu_up",
    )(x, w_gate, w_up)


def _router_kernel(x_ref, w_ref, o_ref):
    logits = _dot(x_ref[...], w_ref[...].astype(BF16))
    lane = lax.broadcasted_iota(I32, logits.shape, 1)
    logits = jnp.where(lane < N_EXPERTS, logits, -jnp.inf)
    t1 = jnp.max(logits, axis=-1, keepdims=True)
    i1 = jnp.min(jnp.where(logits == t1, lane, LANES), axis=-1, keepdims=True)
    rest = jnp.where(lane == i1, -jnp.inf, logits)
    t2 = jnp.max(rest, axis=-1, keepdims=True)
    i2 = jnp.min(jnp.where(rest == t2, lane, LANES), axis=-1, keepdims=True)
    e2 = jnp.exp(t2 - t1)
    den = 1.0 + e2
    o_ref[...] = jnp.where(lane == ROUTE_E1, i1.astype(F32), 0.0) + jnp.where(lane == ROUTE_E2, i2.astype(F32), 0.0) \
        + jnp.where(lane == ROUTE_W1, 1.0 / den, 0.0) + jnp.where(lane == ROUTE_W2, e2 / den, 0.0)


def moe_router(x, w_router, tm):
    m, k = x.shape
    w_pad = jnp.pad(w_router, ((0, 0), (0, LANES - w_router.shape[1])))
    return pl.pallas_call(
        _router_kernel,
        grid=(m // tm,),
        in_specs=[pl.BlockSpec((tm, k), lambda i: (i, 0)), pl.BlockSpec((k, LANES), lambda i: (0, 0))],
        out_specs=pl.BlockSpec((tm, LANES), lambda i: (i, 0)),
        out_shape=jax.ShapeDtypeStruct((m, LANES), F32),
        compiler_params=_cparams("parallel"),
        name="moe_router",
    )(x, w_pad)


def _rmsnorm_pack_kernel(x_ref, g_ref, o_ref, p_ref):
    x = x_ref[...]
    ms = jnp.mean(x * x, axis=-1, keepdims=True)
    y = ((x * lax.rsqrt(ms + EPS)) * g_ref[...]).astype(BF16)
    o_ref[...] = y
    bits = pltpu.bitcast(y.astype(F32), I32)
    half = x.shape[1] // 2
    p_ref[...] = (bits[:, half:] & HI16) | lax.shift_right_logical(bits[:, :half], 16)


def rmsnorm_pack(x, g, tm):
    n, d = x.shape
    return pl.pallas_call(
        _rmsnorm_pack_kernel,
        grid=(n // tm,),
        in_specs=[pl.BlockSpec((tm, d), lambda i: (i, 0)), pl.BlockSpec((1, d), lambda i: (0, 0))],
        out_specs=[pl.BlockSpec((tm, d), lambda i: (i, 0)), pl.BlockSpec((tm, d // 2), lambda i: (i, 0))],
        out_shape=[jax.ShapeDtypeStruct((n, d), BF16), jax.ShapeDtypeStruct((n, d // 2), I32)],
        compiler_params=_cparams("parallel"),
        name="rmsnorm_pack",
    )(x, g.reshape(1, d))


def _unpack_bf16(p):
    lo = pltpu.bitcast(lax.shift_left(p, 16), F32).astype(BF16)
    hi = pltpu.bitcast(p & HI16, F32).astype(BF16)
    return lo, hi


def moe_plan(route, tm, m_pad):
    n = route.shape[0]
    ef = route[:, ROUTE_E1:ROUTE_E2 + 1].astype(I32).reshape(2 * n)
    onehot = (ef[:, None] == jnp.arange(N_EXPERTS, dtype=I32)[None, :]).astype(I32)
    counts = jnp.sum(onehot, axis=0)
    rank = jnp.sum((jnp.cumsum(onehot, axis=0) - onehot) * onehot, axis=1)
    padded = ((counts + tm - 1) // tm) * tm
    ends = jnp.cumsum(padded)
    dest = jnp.sum(onehot * (ends - padded)[None, :], axis=1) + rank
    tile_start = jnp.arange(m_pad // tm, dtype=I32) * tm
    tile_expert = jnp.minimum(jnp.sum((tile_start[:, None] >= ends[None, :]).astype(I32), axis=1), N_EXPERTS - 1)
    n_valid = (ends[-1] // tm).reshape(1)
    return dest, tile_expert, n_valid


def _dispatch_kernel(dest_ref, x_ref, xs_in_ref, xs_ref, sem):
    del xs_in_ref
    rows = x_ref.shape[0]

    def row_copy(r, s):
        d = dest_ref[0, 0, TOP_K * r + s]
        return pltpu.make_async_copy(x_ref.at[pl.ds(r, 1), :], xs_ref.at[pl.ds(d, 1), :], sem)

    def start(r, c):
        for s in range(TOP_K):
            row_copy(r, s).start()
        return c

    def wait(r, c):
        for s in range(TOP_K):
            row_copy(r, s).wait()
        return c

    lax.fori_loop(0, rows, start, 0)
    lax.fori_loop(0, rows, wait, 0)


def moe_dispatch(xp, dest, m_pad, tr):
    n, w = xp.shape
    return pl.pallas_call(
        _dispatch_kernel,
        grid=(n // tr,),
        in_specs=[
            pl.BlockSpec((1, 1, TOP_K * tr), lambda i: (i, 0, 0), memory_space=pltpu.SMEM),
            pl.BlockSpec((tr, w), lambda i: (i, 0)),
            pl.BlockSpec(memory_space=pl.ANY),
        ],
        out_specs=pl.BlockSpec(memory_space=pl.ANY),
        out_shape=jax.ShapeDtypeStruct((m_pad, w), I32),
        scratch_shapes=[pltpu.SemaphoreType.DMA(())],
        input_output_aliases={2: 0},
        compiler_params=_cparams("arbitrary"),
        name="moe_dispatch",
    )(dest.reshape(n // tr, 1, TOP_K * tr), xp, jnp.zeros((m_pad, w), I32))


def _moe_up_kernel(te_ref, nv_ref, x_ref, wg_ref, wu_ref, o_ref):
    del te_ref
    valid = pl.program_id(1) < nv_ref[0]

    @pl.when(valid)
    def _():
        lo, hi = _unpack_bf16(x_ref[...])
        half = x_ref.shape[1]
        a = _dot(lo, wg_ref[:half, :].astype(BF16)) + _dot(hi, wg_ref[half:, :].astype(BF16))
        b = _dot(lo, wu_ref[:half, :].astype(BF16)) + _dot(hi, wu_ref[half:, :].astype(BF16))
        o_ref[...] = (a * jax.nn.sigmoid(a) * b).astype(BF16)

    @pl.when(jnp.logical_not(valid))
    def _():
        o_ref[...] = jnp.zeros_like(o_ref)


def moe_up(xs, w_gate, w_up, tile_expert, n_valid, tm, tn):
    m = xs.shape[0]
    _, k, f = w_gate.shape
    grid_spec = pltpu.PrefetchScalarGridSpec(
        num_scalar_prefetch=2,
        grid=(f // tn, m // tm),
        in_specs=[
            pl.BlockSpec((tm, k // 2), lambda j, i, te, nv: (i, 0)),
            pl.BlockSpec((None, k, tn), lambda j, i, te, nv: (te[i], 0, j)),
            pl.BlockSpec((None, k, tn), lambda j, i, te, nv: (te[i], 0, j)),
        ],
        out_specs=pl.BlockSpec((tm, tn), lambda j, i, te, nv: (i, j)),
    )
    return pl.pallas_call(
        _moe_up_kernel,
        grid_spec=grid_spec,
        out_shape=jax.ShapeDtypeStruct((m, f), BF16),
        compiler_params=_cparams("parallel", "arbitrary"),
        name="moe_up",
    )(tile_expert, n_valid, xs, w_gate, w_up)


def _moe_down_kernel(te_ref, nv_ref, a_ref, w_ref, o_ref):
    del te_ref
    valid = pl.program_id(1) < nv_ref[0]

    @pl.when(valid)
    def _():
        o_ref[...] = _dot(a_ref[...], w_ref[...].astype(BF16))

    @pl.when(jnp.logical_not(valid))
    def _():
        o_ref[...] = jnp.zeros_like(o_ref)


def moe_down(a, w_down, tile_expert, n_valid, tm, tn):
    m, f = a.shape
    d = w_down.shape[2]
    grid_spec = pltpu.PrefetchScalarGridSpec(
        num_scalar_prefetch=2,
        grid=(d // tn, m // tm),
        in_specs=[
            pl.BlockSpec((tm, f), lambda j, i, te, nv: (i, 0)),
            pl.BlockSpec((None, f, tn), lambda j, i, te, nv: (te[i], 0, j)),
        ],
        out_specs=pl.BlockSpec((tm, tn), lambda j, i, te, nv: (i, j)),
    )
    return pl.pallas_call(
        _moe_down_kernel,
        grid_spec=grid_spec,
        out_shape=jax.ShapeDtypeStruct((m, d), F32),
        compiler_params=_cparams("parallel", "arbitrary"),
        name="moe_down",
    )(tile_expert, n_valid, a, w_down)


def _combine_kernel(dest_ref, y_ref, h_ref, rt_ref, o_ref, buf_ref, sem):
    rows = h_ref.shape[0]

    def row_copy(r, s):
        d = dest_ref[0, 0, TOP_K * r + s]
        return pltpu.make_async_copy(y_ref.at[pl.ds(d, 1), :], buf_ref.at[s, pl.ds(r, 1), :], sem.at[s])

    def start(r, c):
        for s in range(TOP_K):
            row_copy(r, s).start()
        return c

    def wait(r, c):
        for s in range(TOP_K):
            row_copy(r, s).wait()
        return c

    lax.fori_loop(0, rows, start, 0)
    lax.fori_loop(0, rows, wait, 0)
    rt = rt_ref[...]
    o_ref[...] = h_ref[...] + (rt[:, ROUTE_W1:ROUTE_W1 + 1] * buf_ref[0] + rt[:, ROUTE_W2:ROUTE_W2 + 1] * buf_ref[1])


def moe_combine(y, dest, route, h, tr):
    n, d = h.shape
    return pl.pallas_call(
        _combine_kernel,
        grid=(n // tr,),
        in_specs=[
            pl.BlockSpec((1, 1, TOP_K * tr), lambda i: (i, 0, 0), memory_space=pltpu.SMEM),
            pl.BlockSpec(memory_space=pl.ANY),
            pl.BlockSpec((tr, d), lambda i: (i, 0)),
            pl.BlockSpec((tr, LANES), lambda i: (i, 0)),
        ],
        out_specs=pl.BlockSpec((tr, d), lambda i: (i, 0)),
        out_shape=jax.ShapeDtypeStruct((n, d), F32),
        scratch_shapes=[pltpu.VMEM((TOP_K, tr, d), F32), pltpu.SemaphoreType.DMA((TOP_K,))],
        compiler_params=_cparams("arbitrary"),
        name="moe_combine",
    )(dest.reshape(n // tr, 1, TOP_K * tr), y, h, route)


def _prep_kernel(q_ref, k_ref, v_ref, qi_ref, sm_ref, c128_ref, s128_ref, c64_ref, s64_ref,
                 gq_ref, gk_ref, gki_ref,
                 qo_ref, kf_ref, kb_ref, vb_ref, qio_ref, kif_ref, kib_ref):
    c128, s128 = c128_ref[...], s128_ref[...]
    c64, s64 = c64_ref[...], s64_ref[...]
    lane = lax.broadcasted_iota(I32, c64.shape, 1)
    lane_lo = (lane & 63) < 32

    def rope128(x):
        return x * c128 + pltpu.roll(x, 64, 1) * s128

    def rope64(x):
        partner = jnp.where(lane_lo, pltpu.roll(x, 96, 1), pltpu.roll(x, 32, 1))
        return x * c64 + partner * s64

    def norm128(x, g):
        ms = jnp.mean(x * x, axis=-1, keepdims=True)
        return x * lax.rsqrt(ms + EPS) * g

    gq, gk = gq_ref[...], gk_ref[...]
    for h in range(DSA_HEADS):
        sl = slice(h * HEAD_DIM, (h + 1) * HEAD_DIM)
        qo_ref[:, sl] = (rope128(norm128(q_ref[:, sl], gq)) * Q_SCALE).astype(BF16)
    for h in range(DSA_KV):
        sl = slice(h * HEAD_DIM, (h + 1) * HEAD_DIM)
        kr = rope128(norm128(k_ref[:, sl], gk))
        kf_ref[:, sl] = kr
        kb_ref[:, sl] = kr.astype(BF16)
    ones = jnp.ones((v_ref.shape[0], HEAD_DIM), BF16)
    for h in range(DSA_KV):
        vb_ref[:, 2 * h * HEAD_DIM:(2 * h + 1) * HEAD_DIM] = v_ref[:, h * HEAD_DIM:(h + 1) * HEAD_DIM].astype(BF16)
        vb_ref[:, (2 * h + 1) * HEAD_DIM:(2 * h + 2) * HEAD_DIM] = ones
    for t in range(IDX_HEADS * IDX_DIM // LANES):
        sl = slice(t * LANES, (t + 1) * LANES)
        qio_ref[:, sl] = rope64(qi_ref[:, sl]).astype(BF16)
    sm = sm_ref[...]
    ms = jnp.sum(jnp.where(lane < IDX_DIM, sm * sm, 0.0), axis=-1, keepdims=True) * (1.0 / IDX_DIM)
    ki = rope64(sm * lax.rsqrt(ms + EPS) * gki_ref[...])
    kif_ref[...] = ki[:, :IDX_DIM]
    kib_ref[...] = ki[:, :IDX_DIM].astype(BF16)


def dsa_prep(proj, tabs, g_q, g_k, g_ki, tm):
    n = proj.shape[0]
    c128, s128, c64, s64 = tabs
    gki_pad = jnp.pad(g_ki, (0, LANES - IDX_DIM)).reshape(1, LANES)
    row = lambda w, off: pl.BlockSpec((tm, w), lambda i: (i, off // w))
    tab = pl.BlockSpec((tm, LANES), lambda i: (i, 0))
    vec = pl.BlockSpec((1, LANES), lambda i: (0, 0))
    out = lambda w: pl.BlockSpec((tm, w), lambda i: (i, 0))
    kvw = DSA_KV * HEAD_DIM
    return pl.pallas_call(
        _prep_kernel,
        grid=(n // tm,),
        in_specs=[row(MIX_B, OFF_DSA_Q), row(kvw, OFF_DSA_K), row(kvw, OFF_DSA_V),
                  row(IDX_HEADS * IDX_DIM, OFF_IDX_Q), row(LANES, OFF_SMALL),
                  tab, tab, tab, tab, vec, vec, vec],
        out_specs=[out(MIX_B), out(kvw), out(kvw), out(2 * kvw), out(IDX_HEADS * IDX_DIM),
                   out(IDX_DIM), out(IDX_DIM)],
        out_shape=[
            jax.ShapeDtypeStruct((n, MIX_B), BF16),
            jax.ShapeDtypeStruct((n, kvw), F32),
            jax.ShapeDtypeStruct((n, kvw), BF16),
            jax.ShapeDtypeStruct((n, 2 * kvw), BF16),
            jax.ShapeDtypeStruct((n, IDX_HEADS * IDX_DIM), BF16),
            jax.ShapeDtypeStruct((n, IDX_DIM), F32),
            jax.ShapeDtypeStruct((n, IDX_DIM), BF16),
        ],
        compiler_params=_cparams("parallel"),
        name="dsa_prep",
    )(proj, proj, proj, proj, proj, c128, s128, c64, s64,
      g_q.reshape(1, LANES), g_k.reshape(1, LANES), gki_pad)


def rope_tables(pos):
    def tab(half, reps):
        inv = ROPE_THETA ** (-jnp.arange(half, dtype=F32) / half)
        ang = pos.astype(F32)[:, None] * inv[None, :]
        cos, sin = jnp.cos(ang), jnp.sin(ang)
        return jnp.tile(jnp.concatenate([cos, cos], -1), (1, reps)), jnp.tile(jnp.concatenate([-sin, sin], -1), (1, reps))
    c128, s128 = tab(HEAD_DIM // 2, 1)
    c64, s64 = tab(IDX_DIM // 2, 2)
    return c128, s128, c64, s64


def _gla_kernel(q_ref, k_ref, v_ref, r_ref, sm_ref, wg2_ref, bg_ref, gout_ref, s0_ref,
                o_ref, sout_ref, st_ref, *, chunk, n_chunks):
    t = pl.program_id(1)

    @pl.when(t == 0)
    def _():
        for h in range(GLA_HEADS):
            st_ref[h] = s0_ref[h].T

    row = lax.broadcasted_iota(I32, (chunk, chunk), 0)
    col = lax.broadcasted_iota(I32, (chunk, chunk), 1)
    causal = row >= col
    tril = jnp.where(causal, 1.0, 0.0).astype(BF16)
    gout = gout_ref[...]

    heads = range(GLA_HEADS)
    dks = [slice(h * GLA_DK, (h + 1) * GLA_DK) for h in heads]
    dvs = [slice(h * GLA_DV, (h + 1) * GLA_DV) for h in heads]
    ws = [_split3(wg2_ref[h]) for h in heads]
    bgs = [bg_ref[:, dks[h]] for h in heads]
    sts = [st_ref[h] for h in heads]
    for c in range(n_chunks):
        sl = pl.ds(c * chunk, chunk)
        s1, s2, s3 = _split3(sm_ref[sl, :])
        qs = [q_ref[sl, dks[h]] * (GLA_DK ** -0.5) for h in heads]
        ks = [k_ref[sl, dks[h]] for h in heads]
        vs = [v_ref[sl, dvs[h]].astype(BF16) for h in heads]
        rs = [r_ref[sl, dvs[h]] for h in heads]
        zs = [(_dot(s1, w[0]) + (_dot(s1, w[1]) + _dot(s2, w[0]))
               + (_dot(s1, w[2]) + _dot(s2, w[1]) + _dot(s3, w[0]))) + bgs[h] for h, w in enumerate(ws)]
        gs = [_split3((jnp.minimum(z, 0.0) - jnp.log1p(jnp.exp(-jnp.abs(z)))) * (1.0 / GLA_TAU)) for z in zs]
        bs = [_dot(tril, g[0]) + _dot(tril, g[1]) + _dot(tril, g[2]) for g in gs]
        b_last = [b[chunk - 1:chunk, :] for b in bs]
        qe = [(qs[h] * jnp.exp(bs[h])).astype(BF16) for h in heads]
        ke = [(ks[h] * jnp.exp(-bs[h])).astype(BF16) for h in heads]
        kd = [(ks[h] * jnp.exp(b_last[h] - bs[h])).astype(BF16) for h in heads]
        a = [jnp.where(causal, _dot_nt(qe[h], ke[h]), 0.0).astype(BF16) for h in heads]
        o = [_dot(a[h], vs[h]) + _dot_nt(qe[h], sts[h].astype(BF16)) for h in heads]
        sts = [jnp.exp(b_last[h]) * sts[h] + _dot_tn(vs[h], kd[h]) for h in heads]
        for h in heads:
            ms = jnp.mean(o[h] * o[h], axis=-1, keepdims=True)
            on = o[h] * lax.rsqrt(ms + EPS) * gout
            o_ref[sl, dvs[h]] = (rs[h] * jax.nn.sigmoid(rs[h]) * on).astype(BF16)
    for h in heads:
        st_ref[h] = sts[h]

    @pl.when(t == pl.num_programs(1) - 1)
    def _():
        for h in range(GLA_HEADS):
            sout_ref[h] = st_ref[h].T


def gla_scan(proj, wg2_pad, b_g, g_out, s0, row0, seq, chunk, tc):
    nb = s0.shape[0]
    nt = seq // tc
    rb0 = row0 // tc
    hk, hv = GLA_HEADS * GLA_DK, GLA_HEADS * GLA_DV
    rows = lambda w, off: pl.BlockSpec((tc, w), lambda b, t: (rb0 + b * nt + t, off // w))
    state = pl.BlockSpec((None, GLA_HEADS, GLA_DK, GLA_DV), lambda b, t: (b, 0, 0, 0))
    kern = functools.partial(_gla_kernel, chunk=chunk, n_chunks=tc // chunk)
    return pl.pallas_call(
        kern,
        grid=(nb, nt),
        in_specs=[
            rows(hk, OFF_GLA_Q), rows(hk, OFF_GLA_K), rows(hv, OFF_GLA_V), rows(hv, OFF_GLA_R),
            rows(LANES, OFF_SMALL),
            pl.BlockSpec((GLA_HEADS, LANES, GLA_DK), lambda b, t: (0, 0, 0)),
            pl.BlockSpec((1, hk), lambda b, t: (0, 0)),
            pl.BlockSpec((1, GLA_DV), lambda b, t: (0, 0)),
            state,
        ],
        out_specs=[pl.BlockSpec((tc, hv), lambda b, t: (b * nt + t, 0)), state],
        out_shape=[
            jax.ShapeDtypeStruct((nb * seq, MIX_A), BF16),
            jax.ShapeDtypeStruct((nb, GLA_HEADS, GLA_DK, GLA_DV), F32),
        ],
        scratch_shapes=[pltpu.VMEM((GLA_HEADS, GLA_DV, GLA_DK), F32)],
        compiler_params=_cparams("parallel", "arbitrary"),
        name="gla_scan",
    )(proj, proj, proj, proj, proj, wg2_pad, b_g.reshape(1, -1), g_out.reshape(1, -1), s0)


def gla_gate_weight(w_g2):
    w = w_g2.reshape(GLA_RANK, GLA_HEADS, GLA_DK).transpose(1, 0, 2)
    return jnp.pad(w, ((0, 0), (SM_GLR, LANES - SM_GLR - GLA_RANK), (0, 0)))


def _dsa_kernel(q_ref, qi_ref, sm_ref, k_ref, v_ref, ki_ref, o_ref,
                key_scr, vec_scr, qis_scr, qg_scr, m_scr, a_scr, p_scr, acc_scr,
                *, qb, tk, lp, n_sel, l_valid, causal_blocks, qpos_static):
    gq = DSA_GROUP * qb
    if causal_blocks:
        i = pl.program_id(1)
        qpos0 = i * qb
        nt = ((i + 1) * qb + tk - 1) // tk
    else:
        qpos0 = qpos_static
        nt = lp // tk
    nsub = tk // LANES
    idx_bits = max(1, (lp - 1).bit_length())

    for h in range(IDX_HEADS):
        qis_scr[h * qb:(h + 1) * qb, :] = qi_ref[:, h * IDX_DIM:(h + 1) * IDX_DIM]
    for g in range(DSA_KV):
        for hh in range(DSA_GROUP):
            hd = g * DSA_GROUP + hh
            qg_scr[g, hh * qb:(hh + 1) * qb, :] = q_ref[:, hd * HEAD_DIM:(hd + 1) * HEAD_DIM]
    wq = sm_ref[:, SM_IDX_W:SM_IDX_W + IDX_HEADS] * ((IDX_DIM ** -0.5) * (IDX_HEADS ** -0.5))
    w_cols = [wq[:, h:h + 1] for h in range(IDX_HEADS)]
    row_chunk = (qpos0 + lax.broadcasted_iota(I32, (qb, tk), 0)) >> 6
    lane_t = lax.broadcasted_iota(I32, (qb, tk), 1)

    def p1(j, carry):
        ks = pl.multiple_of(j * tk, tk)
        s_all = _dot_nt(qis_scr[...], ki_ref[pl.ds(ks, tk), :])
        acc = jnp.zeros((qb, tk), F32)
        for h in range(IDX_HEADS):
            acc = acc + jnp.maximum(s_all[h * qb:(h + 1) * qb, :], 0.0) * w_cols[h]
        kpos = ks + lane_t
        adm = jnp.where((kpos >> 6) <= row_chunk, kpos, l_valid) < l_valid
        score = jnp.where(adm, acc, -jnp.inf)
        bits = pltpu.bitcast(score, I32)
        key_scr[:, pl.ds(ks, tk)] = bits ^ ((bits >> 31) & 0x7FFFFFFF)
        return carry

    lax.fori_loop(0, nt, p1, 0)

    rh = min(qb, 128)

    row_groups = list(range(0, qb, rh))

    def count(pred, *row_vecs):
        for i, v in enumerate(row_vecs):
            vec_scr[i] = jnp.broadcast_to(v, (qb, LANES))

        def body(j, accs):
            accs = list(accs)
            for gi, r0 in enumerate(row_groups):
                vecs = [vec_scr[i, r0:r0 + rh, :] for i in range(len(row_vecs))]
                for c in range(nsub):
                    off = pl.multiple_of(j * tk + c * LANES, LANES)
                    blk = key_scr[r0:r0 + rh, pl.ds(off, LANES)]
                    idx = off + lax.broadcasted_iota(I32, (rh, LANES), 1)
                    accs[gi] = accs[gi] + jnp.where(pred(blk, idx, *vecs), 1, 0)
            return tuple(accs)

        accs = lax.fori_loop(0, nt, body, tuple(jnp.zeros((rh, LANES), I32) for _ in row_groups))
        acc = accs[0] if len(accs) == 1 else jnp.concatenate(accs, axis=0)
        return jnp.sum(acc, axis=1, keepdims=True)

    c0 = count(lambda blk, idx: blk >= 0)
    thr = jnp.where(c0 >= n_sel, 0, INT_MIN).astype(I32)

    def bit_step(it, thr):
        cand = thr + lax.shift_left(jnp.int32(1), 30 - it)
        cnt = count(lambda blk, idx, c: blk >= c, cand)
        return jnp.where(cnt >= n_sel, cand, thr)

    thr = lax.fori_loop(0, 31, bit_step, thr)
    cnt_ge = count(lambda blk, idx, t: blk >= t, thr)
    tie_rows = jnp.where(thr > KEY_NEG_INF, cnt_ge, 0) > n_sel
    any_tie = jnp.max(jnp.where(tie_rows, 1, 0)) > 0

    def tie_cut():
        need = n_sel - count(lambda blk, idx, t: blk > t, thr)
        pos = jnp.zeros((qb, 1), I32)
        for b in range(idx_bits - 1, -1, -1):
            cand = pos + (1 << b)
            f = count(lambda blk, idx, t, c: jnp.where(blk == t, idx, lp) < c, thr, cand)
            pos = jnp.where(f < need, cand, pos)
        return pos

    cut = lax.cond(any_tie, tie_cut, lambda: jnp.full((qb, 1), lp, I32))

    m_scr[...] = jnp.full(m_scr.shape, M_INIT, F32)
    acc_scr[...] = jnp.zeros(acc_scr.shape, F32)
    rb = min(qb, 32)

    def p3(j, carry):
        ks = pl.multiple_of(j * tk, tk)
        key = key_scr[:, pl.ds(ks, tk)]
        idx = ks + lane_t
        eq_ok = jnp.where(key == thr, idx, lp + 1) <= cut
        bias = jnp.where(key > thr, 0.0, jnp.where(eq_ok, 0.0, MASK_BIAS))
        bias = jnp.where(key > KEY_NEG_INF, bias, MASK_BIAS)
        s_all = [_dot_nt(qg_scr[g], k_ref[pl.ds(ks, tk), g * HEAD_DIM:(g + 1) * HEAD_DIM])
                 for g in range(DSA_KV)]
        for g in range(DSA_KV):
            for r0 in range(0, gq, rb):
                rs = slice(r0, r0 + rb)
                s = s_all[g][rs, :] + bias[r0 % qb:r0 % qb + rb, :]
                m_old = m_scr[g, rs, :]
                m_new = jnp.maximum(m_old, jnp.max(s, axis=1, keepdims=True))
                p_scr[g, rs, :] = jnp.exp2(s - m_new).astype(BF16)
                a_scr[g, rs, :] = jnp.exp2(m_old - m_new)
                m_scr[g, rs, :] = m_new
        for g in range(DSA_KV):
            vt = v_ref[pl.ds(ks, tk), 2 * g * HEAD_DIM:(2 * g + 2) * HEAD_DIM]
            acc_scr[g] = a_scr[g] * acc_scr[g] + _dot(p_scr[g], vt)
        return carry

    lax.fori_loop(0, nt, p3, 0)

    for g in range(DSA_KV):
        acc = acc_scr[g]
        o = acc[:, :HEAD_DIM] / acc[:, HEAD_DIM:]
        for hh in range(DSA_GROUP):
            hd = g * DSA_GROUP + hh
            o_ref[:, hd * HEAD_DIM:(hd + 1) * HEAD_DIM] = o[hh * qb:(hh + 1) * qb, :].astype(BF16)


def dsa_attention(q, qi, proj, k, v, ki, *, row0, nq, qb, tk, n_sel, l_valid, causal_blocks, qpos_static):
    nb, lp, kvw = k.shape
    rb0 = row0 // qb
    gq = DSA_GROUP * qb
    kern = functools.partial(_dsa_kernel, qb=qb, tk=tk, lp=lp, n_sel=n_sel, l_valid=l_valid,
                             causal_blocks=causal_blocks, qpos_static=qpos_static)
    return pl.pallas_call(
        kern,
        grid=(nb, nq),
        in_specs=[
            pl.BlockSpec((qb, MIX_B), lambda b, i: (rb0 + b * nq + i, 0)),
            pl.BlockSpec((qb, IDX_HEADS * IDX_DIM), lambda b, i: (rb0 + b * nq + i, 0)),
            pl.BlockSpec((qb, LANES), lambda b, i: (rb0 + b * nq + i, OFF_SMALL // LANES)),
            pl.BlockSpec((None, lp, kvw), lambda b, i: (b, 0, 0)),
            pl.BlockSpec((None, lp, 2 * kvw), lambda b, i: (b, 0, 0)),
            pl.BlockSpec((None, lp, IDX_DIM), lambda b, i: (b, 0, 0)),
        ],
        out_specs=pl.BlockSpec((qb, MIX_B), lambda b, i: (b * nq + i, 0)),
        out_shape=jax.ShapeDtypeStruct((nb * nq * qb, MIX_B), BF16),
        scratch_shapes=[
            pltpu.VMEM((qb, lp), I32),
            pltpu.VMEM((2, qb, LANES), I32),
            pltpu.VMEM((IDX_HEADS * qb, IDX_DIM), BF16),
            pltpu.VMEM((DSA_KV, gq, HEAD_DIM), BF16),
            pltpu.VMEM((DSA_KV, gq, 1), F32),
            pltpu.VMEM((DSA_KV, gq, 1), F32),
            pltpu.VMEM((DSA_KV, gq, tk), BF16),
            pltpu.VMEM((DSA_KV, gq, 2 * HEAD_DIM), F32),
        ],
        compiler_params=_cparams("parallel", "arbitrary"),
        name="dsa_attention",
    )(q, qi, proj, k, v, ki)


def permute_w_in(w_in):
    offs, o = {}, 0
    for name, size in COL_SIZES:
        offs[name] = (o, size)
        o += size
    parts = [w_in[:, offs[n][0]:offs[n][0] + offs[n][1]] for n in PERM_ORDER]
    used = sum(offs[n][1] for n in PERM_ORDER)
    parts.append(jnp.zeros((w_in.shape[0], P_COLS - used), w_in.dtype))
    return jnp.concatenate(parts, axis=1)


TM_BIG = 1536
TM_DOWN = 768
TC_GLA = 256
TK_DSA = 512
QB_DSA = 256
SAMPLE_LP = 2176
TM_MOE = 1024
TM_MOE_DOWN = 512
TR_MOE = 256
MOE_ROWS = -(-(TOP_K * N_TOK + N_EXPERTS * (TM_MOE - 1)) // TM_MOE) * TM_MOE


def kernel(x_prompt, x_sample, cache_k, cache_v, cache_ki, state_gla, g_attn, w_in, w_gla_gate2, b_gla_gate,
           g_gla_out, g_q, g_k, g_ki, w_branch_a, w_branch_b, w_out, g_ffn, w_ff_gate, w_ff_up, w_ff_down,
           w_router, w_moe_gate, w_moe_up, w_moe_down):
    h = jnp.concatenate([x_prompt.reshape(N_PROMPT, D_MODEL), x_sample.reshape(N_SAMPLE, D_MODEL)], axis=0)
    pos = jnp.concatenate([jnp.tile(jnp.arange(SEQ, dtype=I32), BATCH),
                           jnp.tile(PAST_LEN + jnp.arange(DEC_SEQ, dtype=I32), DEC_BATCH)])
    tabs = rope_tables(pos)
    kvw = DSA_KV * HEAD_DIM
    l_sample = PAST_LEN + DEC_SEQ
    zeros_state = jnp.zeros((BATCH, GLA_HEADS, GLA_DK, GLA_DV), F32)
    outs = {n: [] for n in ('kp', 'vp', 'kip', 'sp', 'ks', 'vs', 'kis', 'ss')}

    for l in range(DEPTH):
        hn = rmsnorm_cast(h, g_attn[l], TM_DOWN)
        proj = matmul(hn, permute_w_in(w_in[l]), TM_BIG, 512, name="in_proj")
        q_b, k_f, k_b, v_b, qi_b, ki_f, ki_b = dsa_prep(proj, tabs, g_q[l], g_k[l], g_ki[l], 512)

        wg2 = gla_gate_weight(w_gla_gate2[l])
        oa_p, s_p = gla_scan(proj, wg2, b_gla_gate[l], g_gla_out[l], zeros_state, 0, SEQ, CHUNK, TC_GLA)
        oa_s, s_s = gla_scan(proj, wg2, b_gla_gate[l], g_gla_out[l], state_gla[l], N_PROMPT, DEC_SEQ,
                             DEC_SEQ, DEC_SEQ)

        ob_p = dsa_attention(
            q_b, qi_b, proj,
            k_b[:N_PROMPT].reshape(BATCH, SEQ, kvw), v_b[:N_PROMPT].reshape(BATCH, SEQ, 2 * kvw),
            ki_b[:N_PROMPT].reshape(BATCH, SEQ, IDX_DIM),
            row0=0, nq=SEQ // QB_DSA, qb=QB_DSA, tk=TK_DSA, n_sel=min(TOPK_MAX, SEQ // 4), l_valid=SEQ,
            causal_blocks=True, qpos_static=0)

        def with_cache(cache_parts, new, width):
            c = jnp.concatenate([p.astype(BF16) for p in cache_parts], axis=-1)
            n = new[N_PROMPT:].reshape(DEC_BATCH, DEC_SEQ, width)
            pad = jnp.zeros((DEC_BATCH, SAMPLE_LP - l_sample, width), BF16)
            return jnp.concatenate([c, n, pad], axis=1)

        ck, cv = cache_k[l], cache_v[l]
        ones = jnp.ones((DEC_BATCH, PAST_LEN, HEAD_DIM), BF16)
        ob_s = dsa_attention(
            q_b, qi_b, proj,
            with_cache([ck[:, :, g, :] for g in range(DSA_KV)], k_b, kvw),
            with_cache([p for g in range(DSA_KV) for p in (cv[:, :, g, :], ones)], v_b, 2 * kvw),
            with_cache([cache_ki[l]], ki_b, IDX_DIM),
            row0=N_PROMPT, nq=1, qb=DEC_SEQ, tk=SAMPLE_LP, n_sel=min(TOPK_MAX, l_sample // 4),
            l_valid=l_sample, causal_blocks=False, qpos_static=PAST_LEN)

        o_a = jnp.concatenate([oa_p, oa_s], axis=0)
        o_b = jnp.concatenate([ob_p, ob_s], axis=0)
        merged = merge_branches(o_a, o_b, w_branch_a[l], w_branch_b[l], proj, TM_BIG, 512)
        h = matmul(merged, w_out[l], TM_BIG, 512, residual=h, name="out_proj")

        j = l // 2
        if l % 2 == 0:
            hn = rmsnorm_cast(h, g_ffn[l], TM_DOWN)
            a = swiglu_up(hn, w_ff_gate[j], w_ff_up[j], TM_BIG, 512)
            h = matmul(a, w_ff_down[j], TM_DOWN, 512, residual=h, name="ffn_down", weight_resident=True)
        else:
            hn, hn_packed = rmsnorm_pack(h, g_ffn[l], TM_DOWN)
            route = moe_router(hn, w_router[j], TM_DOWN)
            dest, tile_expert, n_valid = moe_plan(route, TM_MOE, MOE_ROWS)
            xs = moe_dispatch(hn_packed, dest, MOE_ROWS, TR_MOE)
            a = moe_up(xs, w_moe_gate[j], w_moe_up[j], tile_expert, n_valid, TM_MOE, 512)
            split = TM_MOE // TM_MOE_DOWN
            y = moe_down(a, w_moe_down[j], jnp.repeat(tile_expert, split), n_valid * split, TM_MOE_DOWN, 512)
            h = moe_combine(y, dest, route, h, TR_MOE)

        v_f = proj[:, OFF_DSA_V:OFF_DSA_V + kvw]
        outs['kp'].append(k_f[:N_PROMPT].reshape(BATCH, SEQ, DSA_KV, HEAD_DIM))
        outs['vp'].append(v_f[:N_PROMPT].reshape(BATCH, SEQ, DSA_KV, HEAD_DIM))
        outs['kip'].append(ki_f[:N_PROMPT].reshape(BATCH, SEQ, IDX_DIM))
        outs['sp'].append(s_p)
        outs['ks'].append(k_f[N_PROMPT:].reshape(DEC_BATCH, DEC_SEQ, DSA_KV, HEAD_DIM))
        outs['vs'].append(v_f[N_PROMPT:].reshape(DEC_BATCH, DEC_SEQ, DSA_KV, HEAD_DIM))
        outs['kis'].append(ki_f[N_PROMPT:].reshape(DEC_BATCH, DEC_SEQ, IDX_DIM))
        outs['ss'].append(s_s)

    st = {n: jnp.stack(v) for n, v in outs.items()}
    return (h[:N_PROMPT].reshape(BATCH, SEQ, D_MODEL), h[N_PROMPT:].reshape(DEC_BATCH, DEC_SEQ, D_MODEL),
            st['kp'], st['vp'], st['kip'], st['sp'], st['ks'], st['vs'], st['kis'], st['ss'])
```

```python
import functools

import jax
import jax.numpy as jnp
from jax import lax
from jax.experimental import pallas as pl
from jax.experimental.pallas import tpu as pltpu

F32 = jnp.float32
BF16 = jnp.bfloat16
I32 = jnp.int32

D_MODEL = 2048
BATCH = 2
SEQ = 8192
DEPTH = 2
DEC_BATCH = 32
DEC_SEQ = 16
PAST_LEN = 2048
CHUNK = 64
QBLK = 128
ROPE_THETA = 10000.0
EPS = 1e-6
GLA_HEADS = 4
GLA_DK = 128
GLA_DV = 256
GLA_RANK = 16
GLA_TAU = 16.0
DSA_HEADS = 8
DSA_KV = 2
DSA_GROUP = DSA_HEADS // DSA_KV
HEAD_DIM = 128
IDX_HEADS = 8
IDX_DIM = 64
TOPK_MAX = 256
MIX_A = GLA_HEADS * GLA_DV
MIX_B = DSA_HEADS * HEAD_DIM
D_FF = 5632
N_EXPERTS = 8
TOP_K = 2

N_PROMPT = BATCH * SEQ
N_SAMPLE = DEC_BATCH * DEC_SEQ
N_TOK = N_PROMPT + N_SAMPLE

LANES = 128
VMEM_LIMIT_BYTES = 56 * 1024 * 1024

COL_SIZES = (
    ('gla_q', 512), ('gla_k', 512), ('gla_v', 1024), ('gla_glr', 16), ('gla_r', 1024),
    ('dsa_q', 1024), ('dsa_k', 256), ('dsa_v', 256), ('idx_q', 512), ('idx_k', 64),
    ('idx_w', 8), ('gate_a', 2048), ('gate_b', 2048),
)
PERM_ORDER = ('gla_q', 'gla_k', 'gla_v', 'gla_r', 'dsa_q', 'dsa_k', 'dsa_v', 'idx_q',
              'gate_a', 'gate_b', 'idx_k', 'gla_glr', 'idx_w')
P_COLS = 9728
OFF_GLA_Q, OFF_GLA_K, OFF_GLA_V, OFF_GLA_R = 0, 512, 1024, 2048
OFF_DSA_Q, OFF_DSA_K, OFF_DSA_V, OFF_IDX_Q = 3072, 4096, 4352, 4608
OFF_GATE_A, OFF_GATE_B, OFF_SMALL = 5120, 7168, 9216
SM_IDX_K, SM_GLR, SM_IDX_W = 0, 64, 80

ROUTE_E1, ROUTE_E2, ROUTE_W1, ROUTE_W2 = 0, 1, 2, 3
HI16 = -65536
KEY_NEG_INF = -2139095041
INT_MIN = -2147483648
MASK_BIAS = -1e30
M_INIT = -5e29
Q_SCALE = (HEAD_DIM ** -0.5) * 1.4426950408889634


def _cparams(*sem):
    return pltpu.CompilerParams(dimension_semantics=sem, vmem_limit_bytes=VMEM_LIMIT_BYTES)


def _dot(a, b):
    return jnp.dot(a, b, preferred_element_type=F32)


def _dot_nt(a, b):
    return lax.dot_general(a, b, (((1,), (1,)), ((), ())), preferred_element_type=F32)


def _dot_tn(a, b):
    return lax.dot_general(a, b, (((0,), (0,)), ((), ())), preferred_element_type=F32)


def _split3(a):
    a1 = a.astype(BF16)
    r1 = a - a1.astype(F32)
    a2 = r1.astype(BF16)
    a3 = (r1 - a2.astype(F32)).astype(BF16)
    return a1, a2, a3


def _rmsnorm_kernel(x_ref, g_ref, o_ref):
    x = x_ref[...]
    ms = jnp.mean(x * x, axis=-1, keepdims=True)
    o_ref[...] = ((x * lax.rsqrt(ms + EPS)) * g_ref[...]).astype(BF16)


def rmsnorm_cast(x, g, tm):
    n, d = x.shape
    return pl.pallas_call(
        _rmsnorm_kernel,
        grid=(n // tm,),
        in_specs=[pl.BlockSpec((tm, d), lambda i: (i, 0)), pl.BlockSpec((1, d), lambda i: (0, 0))],
        out_specs=pl.BlockSpec((tm, d), lambda i: (i, 0)),
        out_shape=jax.ShapeDtypeStruct((n, d), BF16),
        compiler_params=_cparams("parallel"),
        name="rmsnorm_cast",
    )(x, g.reshape(1, d))


def _mm_kernel(x_ref, w_ref, o_ref):
    o_ref[...] = _dot(x_ref[...], w_ref[...].astype(BF16))


def _mm_res_kernel(x_ref, w_ref, r_ref, o_ref):
    o_ref[...] = r_ref[...] + _dot(x_ref[...], w_ref[...].astype(BF16))


def matmul(x, w, tm, tn, residual=None, name="matmul", weight_resident=False):
    m, k = x.shape
    n = w.shape[1]
    if weight_resident:
        grid = (n // tn, m // tm)
        rc = lambda a, b: (b, a)
    else:
        grid = (m // tm, n // tn)
        rc = lambda a, b: (a, b)
    in_specs = [pl.BlockSpec((tm, k), lambda a, b: (rc(a, b)[0], 0)),
                pl.BlockSpec((k, tn), lambda a, b: (0, rc(a, b)[1]))]
    args = [x, w]
    body = _mm_kernel
    if residual is not None:
        in_specs.append(pl.BlockSpec((tm, tn), lambda a, b: rc(a, b)))
        args.append(residual)
        body = _mm_res_kernel
    return pl.pallas_call(
        body,
        grid=grid,
        in_specs=in_specs,
        out_specs=pl.BlockSpec((tm, tn), lambda a, b: rc(a, b)),
        out_shape=jax.ShapeDtypeStruct((m, n), F32),
        compiler_params=_cparams("parallel", "parallel"),
        name=name,
    )(*args)


def _merge_kernel(oa_ref, ob_ref, wa_ref, wb_ref, ga_ref, gb_ref, o_ref):
    a = _dot(oa_ref[...], wa_ref[...].astype(BF16))
    b = _dot(ob_ref[...], wb_ref[...].astype(BF16))
    o_ref[...] = (jax.nn.sigmoid(ga_ref[...]) * a + jax.nn.sigmoid(gb_ref[...]) * b).astype(BF16)


def merge_branches(o_a, o_b, w_pa, w_pb, proj, tm, tn):
    m = o_a.shape[0]
    n = w_pa.shape[1]
    ja, jb = OFF_GATE_A // tn, OFF_GATE_B // tn
    return pl.pallas_call(
        _merge_kernel,
        grid=(m // tm, n // tn),
        in_specs=[
            pl.BlockSpec((tm, MIX_A), lambda i, j: (i, 0)),
            pl.BlockSpec((tm, MIX_B), lambda i, j: (i, 0)),
            pl.BlockSpec((MIX_A, tn), lambda i, j: (0, j)),
            pl.BlockSpec((MIX_B, tn), lambda i, j: (0, j)),
            pl.BlockSpec((tm, tn), lambda i, j: (i, ja + j)),
            pl.BlockSpec((tm, tn), lambda i, j: (i, jb + j)),
        ],
        out_specs=pl.BlockSpec((tm, tn), lambda i, j: (i, j)),
        out_shape=jax.ShapeDtypeStruct((m, n), BF16),
        compiler_params=_cparams("parallel", "parallel"),
        name="merge_branches",
    )(o_a, o_b, w_pa, w_pb, proj, proj)


def _swiglu_kernel(x_ref, wg_ref, wu_ref, o_ref):
    x = x_ref[...]
    a = _dot(x, wg_ref[...].astype(BF16))
    b = _dot(x, wu_ref[...].astype(BF16))
    o_ref[...] = (a * jax.nn.sigmoid(a) * b).astype(BF16)


def swiglu_up(x, w_gate, w_up, tm, tn):
    m, k = x.shape
    f = w_gate.shape[1]
    return pl.pallas_call(
        _swiglu_kernel,
        grid=(m // tm, f // tn),
        in_specs=[
            pl.BlockSpec((tm, k), lambda i, j: (i, 0)),
            pl.BlockSpec((k, tn), lambda i, j: (0, j)),
            pl.BlockSpec((k, tn), lambda i, j: (0, j)),
        ],
        out_specs=pl.BlockSpec((tm, tn), lambda i, j: (i, j)),
        out_shape=jax.ShapeDtypeStruct((m, f), BF16),
        compiler_params=_cparams("parallel", "parallel"),
        name="swiglu_up",
    )(x, w_gate, w_up)


def _router_kernel(x_ref, w_ref, o_ref):
    logits = _dot(x_ref[...], w_ref[...].astype(BF16))
    lane = lax.broadcasted_iota(I32, logits.shape, 1)
    logits = jnp.where(lane < N_EXPERTS, logits, -jnp.inf)
    t1 = jnp.max(logits, axis=-1, keepdims=True)
    i1 = jnp.min(jnp.where(logits == t1, lane, LANES), axis=-1, keepdims=True)
    rest = jnp.where(lane == i1, -jnp.inf, logits)
    t2 = jnp.max(rest, axis=-1, keepdims=True)
    i2 = jnp.min(jnp.where(rest == t2, lane, LANES), axis=-1, keepdims=True)
    e2 = jnp.exp(t2 - t1)
    den = 1.0 + e2
    o_ref[...] = jnp.where(lane == ROUTE_E1, i1.astype(F32), 0.0) + jnp.where(lane == ROUTE_E2, i2.astype(F32), 0.0) \
        + jnp.where(lane == ROUTE_W1, 1.0 / den, 0.0) + jnp.where(lane == ROUTE_W2, e2 / den, 0.0)


def moe_router(x, w_router, tm):
    m, k = x.shape
    w_pad = jnp.pad(w_router, ((0, 0), (0, LANES - w_router.shape[1])))
    return pl.pallas_call(
        _router_kernel,
        grid=(m // tm,),
        in_specs=[pl.BlockSpec((tm, k), lambda i: (i, 0)), pl.BlockSpec((k, LANES), lambda i: (0, 0))],
        out_specs=pl.BlockSpec((tm, LANES), lambda i: (i, 0)),
        out_shape=jax.ShapeDtypeStruct((m, LANES), F32),
        compiler_params=_cparams("parallel"),
        name="moe_router",
    )(x, w_pad)


def _rmsnorm_pack_kernel(x_ref, g_ref, o_ref, p_ref):
    x = x_ref[...]
    ms = jnp.mean(x * x, axis=-1, keepdims=True)
    y = ((x * lax.rsqrt(ms + EPS)) * g_ref[...]).astype(BF16)
    o_ref[...] = y
    bits = pltpu.bitcast(y.astype(F32), I32)
    half = x.shape[1] // 2
    p_ref[...] = (bits[:, half:] & HI16) | lax.shift_right_logical(bits[:, :half], 16)


def rmsnorm_pack(x, g, tm):
    n, d = x.shape
    return pl.pallas_call(
        _rmsnorm_pack_kernel,
        grid=(n // tm,),
        in_specs=[pl.BlockSpec((tm, d), lambda i: (i, 0)), pl.BlockSpec((1, d), lambda i: (0, 0))],
        out_specs=[pl.BlockSpec((tm, d), lambda i: (i, 0)), pl.BlockSpec((tm, d // 2), lambda i: (i, 0))],
        out_shape=[jax.ShapeDtypeStruct((n, d), BF16), jax.ShapeDtypeStruct((n, d // 2), I32)],
        compiler_params=_cparams("parallel"),
        name="rmsnorm_pack",
    )(x, g.reshape(1, d))


def _unpack_bf16(p):
    lo = pltpu.bitcast(lax.shift_left(p, 16), F32).astype(BF16)
    hi = pltpu.bitcast(p & HI16, F32).astype(BF16)
    return lo, hi


def moe_plan(route, tm, m_pad):
    n = route.shape[0]
    ef = route[:, ROUTE_E1:ROUTE_E2 + 1].astype(I32).reshape(2 * n)
    onehot = (ef[:, None] == jnp.arange(N_EXPERTS, dtype=I32)[None, :]).astype(I32)
    counts = jnp.sum(onehot, axis=0)
    rank = jnp.sum((jnp.cumsum(onehot, axis=0) - onehot) * onehot, axis=1)
    padded = ((counts + tm - 1) // tm) * tm
    ends = jnp.cumsum(padded)
    dest = jnp.sum(onehot * (ends - padded)[None, :], axis=1) + rank
    tile_start = jnp.arange(m_pad // tm, dtype=I32) * tm
    tile_expert = jnp.minimum(jnp.sum((tile_start[:, None] >= ends[None, :]).astype(I32), axis=1), N_EXPERTS - 1)
    n_valid = (ends[-1] // tm).reshape(1)
    return dest, tile_expert, n_valid


def _dispatch_kernel(dest_ref, x_ref, xs_in_ref, xs_ref, sem):
    del xs_in_ref
    rows = x_ref.shape[0]

    def row_copy(r, s):
        d = dest_ref[0, 0, TOP_K * r + s]
        return pltpu.make_async_copy(x_ref.at[pl.ds(r, 1), :], xs_ref.at[pl.ds(d, 1), :], sem)

    def start(r, c):
        for s in range(TOP_K):
            row_copy(r, s).start()
        return c

    def wait(r, c):
        for s in range(TOP_K):
            row_copy(r, s).wait()
        return c

    lax.fori_loop(0, rows, start, 0)
    lax.fori_loop(0, rows, wait, 0)


def moe_dispatch(xp, dest, m_pad, tr):
    n, w = xp.shape
    return pl.pallas_call(
        _dispatch_kernel,
        grid=(n // tr,),
        in_specs=[
            pl.BlockSpec((1, 1, TOP_K * tr), lambda i: (i, 0, 0), memory_space=pltpu.SMEM),
            pl.BlockSpec((tr, w), lambda i: (i, 0)),
            pl.BlockSpec(memory_space=pl.ANY),
        ],
        out_specs=pl.BlockSpec(memory_space=pl.ANY),
        out_shape=jax.ShapeDtypeStruct((m_pad, w), I32),
        scratch_shapes=[pltpu.SemaphoreType.DMA(())],
        input_output_aliases={2: 0},
        compiler_params=_cparams("arbitrary"),
        name="moe_dispatch",
    )(dest.reshape(n // tr, 1, TOP_K * tr), xp, jnp.zeros((m_pad, w), I32))


def _moe_up_kernel(te_ref, nv_ref, x_ref, wg_ref, wu_ref, o_ref):
    del te_ref
    valid = pl.program_id(1) < nv_ref[0]

    @pl.when(valid)
    def _():
        lo, hi = _unpack_bf16(x_ref[...])
        half = x_ref.shape[1]
        a = _dot(lo, wg_ref[:half, :].astype(BF16)) + _dot(hi, wg_ref[half:, :].astype(BF16))
        b = _dot(lo, wu_ref[:half, :].astype(BF16)) + _dot(hi, wu_ref[half:, :].astype(BF16))
        o_ref[...] = (a * jax.nn.sigmoid(a) * b).astype(BF16)

    @pl.when(jnp.logical_not(valid))
    def _():
        o_ref[...] = jnp.zeros_like(o_ref)


def moe_up(xs, w_gate, w_up, tile_expert, n_valid, tm, tn):
    m = xs.shape[0]
    _, k, f = w_gate.shape
    grid_spec = pltpu.PrefetchScalarGridSpec(
        num_scalar_prefetch=2,
        grid=(f // tn, m // tm),
        in_specs=[
            pl.BlockSpec((tm, k // 2), lambda j, i, te, nv: (i, 0)),
            pl.BlockSpec((None, k, tn), lambda j, i, te, nv: (te[i], 0, j)),
            pl.BlockSpec((None, k, tn), lambda j, i, te, nv: (te[i], 0, j)),
        ],
        out_specs=pl.BlockSpec((tm, tn), lambda j, i, te, nv: (i, j)),
    )
    return pl.pallas_call(
        _moe_up_kernel,
        grid_spec=grid_spec,
        out_shape=jax.ShapeDtypeStruct((m, f), BF16),
        compiler_params=_cparams("parallel", "arbitrary"),
        name="moe_up",
    )(tile_expert, n_valid, xs, w_gate, w_up)


def _moe_down_kernel(te_ref, nv_ref, a_ref, w_ref, o_ref):
    del te_ref
    valid = pl.program_id(1) < nv_ref[0]

    @pl.when(valid)
    def _():
        o_ref[...] = _dot(a_ref[...], w_ref[...].astype(BF16))

    @pl.when(jnp.logical_not(valid))
    def _():
        o_ref[...] = jnp.zeros_like(o_ref)


def moe_down(a, w_down, tile_expert, n_valid, tm, tn):
    m, f = a.shape
    d = w_down.shape[2]
    grid_spec = pltpu.PrefetchScalarGridSpec(
        num_scalar_prefetch=2,
        grid=(d // tn, m // tm),
        in_specs=[
            pl.BlockSpec((tm, f), lambda j, i, te, nv: (i, 0)),
            pl.BlockSpec((None, f, tn), lambda j, i, te, nv: (te[i], 0, j)),
        ],
        out_specs=pl.BlockSpec((tm, tn), lambda j, i, te, nv: (i, j)),
    )
    return pl.pallas_call(
        _moe_down_kernel,
        grid_spec=grid_spec,
        out_shape=jax.ShapeDtypeStruct((m, d), F32),
        compiler_params=_cparams("parallel", "arbitrary"),
        name="moe_down",
    )(tile_expert, n_valid, a, w_down)


def _combine_kernel(dest_ref, y_ref, h_ref, rt_ref, o_ref, buf_ref, sem):
    rows = h_ref.shape[0]

    def row_copy(r, s):
        d = dest_ref[0, 0, TOP_K * r + s]
        return pltpu.make_async_copy(y_ref.at[pl.ds(d, 1), :], buf_ref.at[s, pl.ds(r, 1), :], sem.at[s])

    def start(r, c):
        for s in range(TOP_K):
            row_copy(r, s).start()
        return c

    def wait(r, c):
        for s in range(TOP_K):
            row_copy(r, s).wait()
        return c

    lax.fori_loop(0, rows, start, 0)
    lax.fori_loop(0, rows, wait, 0)
    rt = rt_ref[...]
    o_ref[...] = h_ref[...] + (rt[:, ROUTE_W1:ROUTE_W1 + 1] * buf_ref[0] + rt[:, ROUTE_W2:ROUTE_W2 + 1] * buf_ref[1])


def moe_combine(y, dest, route, h, tr):
    n, d = h.shape
    return pl.pallas_call(
        _combine_kernel,
        grid=(n // tr,),
        in_specs=[
            pl.BlockSpec((1, 1, TOP_K * tr), lambda i: (i, 0, 0), memory_space=pltpu.SMEM),
            pl.BlockSpec(memory_space=pl.ANY),
            pl.BlockSpec((tr, d), lambda i: (i, 0)),
            pl.BlockSpec((tr, LANES), lambda i: (i, 0)),
        ],
        out_specs=pl.BlockSpec((tr, d), lambda i: (i, 0)),
        out_shape=jax.ShapeDtypeStruct((n, d), F32),
        scratch_shapes=[pltpu.VMEM((TOP_K, tr, d), F32), pltpu.SemaphoreType.DMA((TOP_K,))],
        compiler_params=_cparams("arbitrary"),
        name="moe_combine",
    )(dest.reshape(n // tr, 1, TOP_K * tr), y, h, route)


def _prep_kernel(q_ref, k_ref, v_ref, qi_ref, sm_ref, c128_ref, s128_ref, c64_ref, s64_ref,
                 gq_ref, gk_ref, gki_ref,
                 qo_ref, kf_ref, kb_ref, vb_ref, qio_ref, kif_ref, kib_ref, kt_ref, kit_ref):
    c128, s128 = c128_ref[...], s128_ref[...]
    c64, s64 = c64_ref[...], s64_ref[...]
    lane = lax.broadcasted_iota(I32, c64.shape, 1)
    lane_lo = (lane & 63) < 32

    def rope128(x):
        return x * c128 + pltpu.roll(x, 64, 1) * s128

    def rope64(x):
        partner = jnp.where(lane_lo, pltpu.roll(x, 96, 1), pltpu.roll(x, 32, 1))
        return x * c64 + partner * s64

    def norm128(x, g):
        ms = jnp.mean(x * x, axis=-1, keepdims=True)
        return x * lax.rsqrt(ms + EPS) * g

    gq, gk = gq_ref[...], gk_ref[...]
    for h in range(DSA_HEADS):
        sl = slice(h * HEAD_DIM, (h + 1) * HEAD_DIM)
        qo_ref[:, sl] = (rope128(norm128(q_ref[:, sl], gq)) * Q_SCALE).astype(BF16)
    for h in range(DSA_KV):
        sl = slice(h * HEAD_DIM, (h + 1) * HEAD_DIM)
        kr = rope128(norm128(k_ref[:, sl], gk))
        kf_ref[:, sl] = kr
        kb_ref[:, sl] = kr.astype(BF16)
        kt_ref[sl, :] = kr.T.astype(BF16)
    ones = jnp.ones((v_ref.shape[0], HEAD_DIM), BF16)
    for h in range(DSA_KV):
        vb_ref[:, 2 * h * HEAD_DIM:(2 * h + 1) * HEAD_DIM] = v_ref[:, h * HEAD_DIM:(h + 1) * HEAD_DIM].astype(BF16)
        vb_ref[:, (2 * h + 1) * HEAD_DIM:(2 * h + 2) * HEAD_DIM] = ones
    for t in range(IDX_HEADS * IDX_DIM // LANES):
        sl = slice(t * LANES, (t + 1) * LANES)
        qio_ref[:, sl] = rope64(qi_ref[:, sl]).astype(BF16)
    sm = sm_ref[...]
    ms = jnp.sum(jnp.where(lane < IDX_DIM, sm * sm, 0.0), axis=-1, keepdims=True) * (1.0 / IDX_DIM)
    ki = rope64(sm * lax.rsqrt(ms + EPS) * gki_ref[...])
    kif_ref[...] = ki[:, :IDX_DIM]
    kib_ref[...] = ki[:, :IDX_DIM].astype(BF16)
    kit_ref[...] = ki.T[:IDX_DIM, :].astype(BF16)


def dsa_prep(proj, tabs, g_q, g_k, g_ki, tm):
    n = proj.shape[0]
    c128, s128, c64, s64 = tabs
    gki_pad = jnp.pad(g_ki, (0, LANES - IDX_DIM)).reshape(1, LANES)
    row = lambda w, off: pl.BlockSpec((tm, w), lambda i: (i, off // w))
    tab = pl.BlockSpec((tm, LANES), lambda i: (i, 0))
    vec = pl.BlockSpec((1, LANES), lambda i: (0, 0))
    out = lambda w: pl.BlockSpec((tm, w), lambda i: (i, 0))
    kvw = DSA_KV * HEAD_DIM
    return pl.pallas_call(
        _prep_kernel,
        grid=(n // tm,),
        in_specs=[row(MIX_B, OFF_DSA_Q), row(kvw, OFF_DSA_K), row(kvw, OFF_DSA_V),
                  row(IDX_HEADS * IDX_DIM, OFF_IDX_Q), row(LANES, OFF_SMALL),
                  tab, tab, tab, tab, vec, vec, vec],
        out_specs=[out(MIX_B), out(kvw), out(kvw), out(2 * kvw), out(IDX_HEADS * IDX_DIM),
                   out(IDX_DIM), out(IDX_DIM),
                   pl.BlockSpec((kvw, tm), lambda i: (0, i)), pl.BlockSpec((IDX_DIM, tm), lambda i: (0, i))],
        out_shape=[
            jax.ShapeDtypeStruct((n, MIX_B), BF16),
            jax.ShapeDtypeStruct((n, kvw), F32),
            jax.ShapeDtypeStruct((n, kvw), BF16),
            jax.ShapeDtypeStruct((n, 2 * kvw), BF16),
            jax.ShapeDtypeStruct((n, IDX_HEADS * IDX_DIM), BF16),
            jax.ShapeDtypeStruct((n, IDX_DIM), F32),
            jax.ShapeDtypeStruct((n, IDX_DIM), BF16),
            jax.ShapeDtypeStruct((kvw, n), BF16),
            jax.ShapeDtypeStruct((IDX_DIM, n), BF16),
        ],
        compiler_params=_cparams("parallel"),
        name="dsa_prep",
    )(proj, proj, proj, proj, proj, c128, s128, c64, s64,
      g_q.reshape(1, LANES), g_k.reshape(1, LANES), gki_pad)


def rope_tables(pos):
    def tab(half, reps):
        inv = ROPE_THETA ** (-jnp.arange(half, dtype=F32) / half)
        ang = pos.astype(F32)[:, None] * inv[None, :]
        cos, sin = jnp.cos(ang), jnp.sin(ang)
        return jnp.tile(jnp.concatenate([cos, cos], -1), (1, reps)), jnp.tile(jnp.concatenate([-sin, sin], -1), (1, reps))
    c128, s128 = tab(HEAD_DIM // 2, 1)
    c64, s64 = tab(IDX_DIM // 2, 2)
    return c128, s128, c64, s64


def _gla_kernel(q_ref, k_ref, v_ref, r_ref, sm_ref, wg2_ref, bg_ref, gout_ref, s0_ref,
                o_ref, sout_ref, st_ref, *, chunk, n_chunks):
    t = pl.program_id(1)

    @pl.when(t == 0)
    def _():
        for h in range(GLA_HEADS):
            st_ref[h] = s0_ref[h].T

    row = lax.broadcasted_iota(I32, (chunk, chunk), 0)
    col = lax.broadcasted_iota(I32, (chunk, chunk), 1)
    causal = row >= col
    tril = jnp.where(causal, 1.0, 0.0).astype(BF16)
    gout = gout_ref[...]

    heads = range(GLA_HEADS)
    dks = [slice(h * GLA_DK, (h + 1) * GLA_DK) for h in heads]
    dvs = [slice(h * GLA_DV, (h + 1) * GLA_DV) for h in heads]
    ws = [_split3(wg2_ref[h]) for h in heads]
    bgs = [bg_ref[:, dks[h]] for h in heads]
    sts = [st_ref[h] for h in heads]
    for c in range(n_chunks):
        sl = pl.ds(c * chunk, chunk)
        s1, s2, s3 = _split3(sm_ref[sl, :])
        qs = [q_ref[sl, dks[h]] * (GLA_DK ** -0.5) for h in heads]
        ks = [k_ref[sl, dks[h]] for h in heads]
        vs = [v_ref[sl, dvs[h]].astype(BF16) for h in heads]
        rs = [r_ref[sl, dvs[h]] for h in heads]
        zs = [(_dot(s1, w[0]) + (_dot(s1, w[1]) + _dot(s2, w[0]))
               + (_dot(s1, w[2]) + _dot(s2, w[1]) + _dot(s3, w[0]))) + bgs[h] for h, w in enumerate(ws)]
        gs = [_split3((jnp.minimum(z, 0.0) - jnp.log1p(jnp.exp(-jnp.abs(z)))) * (1.0 / GLA_TAU)) for z in zs]
        bs = [_dot(tril, g[0]) + _dot(tril, g[1]) + _dot(tril, g[2]) for g in gs]
        b_last = [b[chunk - 1:chunk, :] for b in bs]
        qe = [(qs[h] * jnp.exp(bs[h])).astype(BF16) for h in heads]
        ke = [(ks[h] * jnp.exp(-bs[h])).astype(BF16) for h in heads]
        kd = [(ks[h] * jnp.exp(b_last[h] - bs[h])).astype(BF16) for h in heads]
        a = [jnp.where(causal, _dot_nt(qe[h], ke[h]), 0.0).astype(BF16) for h in heads]
        o = [_dot(a[h], vs[h]) + _dot_nt(qe[h], sts[h].astype(BF16)) for h in heads]
        sts = [jnp.exp(b_last[h]) * sts[h] + _dot_tn(vs[h], kd[h]) for h in heads]
        for h in heads:
            ms = jnp.mean(o[h] * o[h], axis=-1, keepdims=True)
            on = o[h] * lax.rsqrt(ms + EPS) * gout
            o_ref[sl, dvs[h]] = (rs[h] * jax.nn.sigmoid(rs[h]) * on).astype(BF16)
    for h in heads:
        st_ref[h] = sts[h]

    @pl.when(t == pl.num_programs(1) - 1)
    def _():
        for h in range(GLA_HEADS):
            sout_ref[h] = st_ref[h].T


def gla_scan(proj, wg2_pad, b_g, g_out, s0, row0, seq, chunk, tc):
    nb = s0.shape[0]
    nt = seq // tc
    rb0 = row0 // tc
    hk, hv = GLA_HEADS * GLA_DK, GLA_HEADS * GLA_DV
    rows = lambda w, off: pl.BlockSpec((tc, w), lambda b, t: (rb0 + b * nt + t, off // w))
    state = pl.BlockSpec((None, GLA_HEADS, GLA_DK, GLA_DV), lambda b, t: (b, 0, 0, 0))
    kern = functools.partial(_gla_kernel, chunk=chunk, n_chunks=tc // chunk)
    return pl.pallas_call(
        kern,
        grid=(nb, nt),
        in_specs=[
            rows(hk, OFF_GLA_Q), rows(hk, OFF_GLA_K), rows(hv, OFF_GLA_V), rows(hv, OFF_GLA_R),
            rows(LANES, OFF_SMALL),
            pl.BlockSpec((GLA_HEADS, LANES, GLA_DK), lambda b, t: (0, 0, 0)),
            pl.BlockSpec((1, hk), lambda b, t: (0, 0)),
            pl.BlockSpec((1, GLA_DV), lambda b, t: (0, 0)),
            state,
        ],
        out_specs=[pl.BlockSpec((tc, hv), lambda b, t: (b * nt + t, 0)), state],
        out_shape=[
            jax.ShapeDtypeStruct((nb * seq, MIX_A), BF16),
            jax.ShapeDtypeStruct((nb, GLA_HEADS, GLA_DK, GLA_DV), F32),
        ],
        scratch_shapes=[pltpu.VMEM((GLA_HEADS, GLA_DV, GLA_DK), F32)],
        compiler_params=_cparams("parallel", "arbitrary"),
        name="gla_scan",
    )(proj, proj, proj, proj, proj, wg2_pad, b_g.reshape(1, -1), g_out.reshape(1, -1), s0)


def gla_gate_weight(w_g2):
    w = w_g2.reshape(GLA_RANK, GLA_HEADS, GLA_DK).transpose(1, 0, 2)
    return jnp.pad(w, ((0, 0), (SM_GLR, LANES - SM_GLR - GLA_RANK), (0, 0)))


def _dsa_kernel(q_ref, qi_ref, sm_ref, k_ref, v_ref, ki_ref, o_ref,
                key_scr, vec_scr, qis_scr, qg_scr, m_scr, a_scr, p_scr, acc_scr,
                *, qb, tk, lp, n_sel, l_valid, causal_blocks, qpos_static, keys_t):
    gq = DSA_GROUP * qb
    if causal_blocks:
        i = pl.program_id(1)
        qpos0 = i * qb
        nt = ((i + 1) * qb + tk - 1) // tk
    else:
        qpos0 = qpos_static
        nt = lp // tk
    nsub = tk // LANES
    idx_bits = max(1, (lp - 1).bit_length())

    for h in range(IDX_HEADS):
        qis_scr[h * qb:(h + 1) * qb, :] = qi_ref[:, h * IDX_DIM:(h + 1) * IDX_DIM]
    for g in range(DSA_KV):
        for hh in range(DSA_GROUP):
            hd = g * DSA_GROUP + hh
            qg_scr[g, hh * qb:(hh + 1) * qb, :] = q_ref[:, hd * HEAD_DIM:(hd + 1) * HEAD_DIM]
    wq = sm_ref[:, SM_IDX_W:SM_IDX_W + IDX_HEADS] * ((IDX_DIM ** -0.5) * (IDX_HEADS ** -0.5))
    w_cols = [wq[:, h:h + 1] for h in range(IDX_HEADS)]
    row_chunk = (qpos0 + lax.broadcasted_iota(I32, (qb, tk), 0)) >> 6
    lane_t = lax.broadcasted_iota(I32, (qb, tk), 1)

    def p1(j, carry):
        ks = pl.multiple_of(j * tk, tk)
        if keys_t:
            s_all = _dot(qis_scr[...], ki_ref[:, pl.ds(ks, tk)])
        else:
            s_all = _dot_nt(qis_scr[...], ki_ref[pl.ds(ks, tk), :])
        acc = jnp.zeros((qb, tk), F32)
        for h in range(IDX_HEADS):
            acc = acc + jnp.maximum(s_all[h * qb:(h + 1) * qb, :], 0.0) * w_cols[h]
        kpos = ks + lane_t
        adm = jnp.where((kpos >> 6) <= row_chunk, kpos, l_valid) < l_valid
        score = jnp.where(adm, acc, -jnp.inf)
        bits = pltpu.bitcast(score, I32)
        key_scr[:, pl.ds(ks, tk)] = bits ^ ((bits >> 31) & 0x7FFFFFFF)
        return carry

    lax.fori_loop(0, nt, p1, 0)

    rh = min(qb, 128)

    row_groups = list(range(0, qb, rh))

    def count(pred, *row_vecs):
        for i, v in enumerate(row_vecs):
            vec_scr[i] = jnp.broadcast_to(v, (qb, LANES))

        def body(j, accs):
            accs = list(accs)
            for gi, r0 in enumerate(row_groups):
                vecs = [vec_scr[i, r0:r0 + rh, :] for i in range(len(row_vecs))]
                for c in range(nsub):
                    off = pl.multiple_of(j * tk + c * LANES, LANES)
                    blk = key_scr[r0:r0 + rh, pl.ds(off, LANES)]
                    idx = off + lax.broadcasted_iota(I32, (rh, LANES), 1)
                    accs[gi] = accs[gi] + jnp.where(pred(blk, idx, *vecs), 1, 0)
            return tuple(accs)

        accs = lax.fori_loop(0, nt, body, tuple(jnp.zeros((rh, LANES), I32) for _ in row_groups))
        acc = accs[0] if len(accs) == 1 else jnp.concatenate(accs, axis=0)
        return jnp.sum(acc, axis=1, keepdims=True)

    c0 = count(lambda blk, idx: blk >= 0)
    thr = jnp.where(c0 >= n_sel, 0, INT_MIN).astype(I32)

    def bit_step(it, thr):
        cand = thr + lax.shift_left(jnp.int32(1), 30 - it)
        cnt = count(lambda blk, idx, c: blk >= c, cand)
        return jnp.where(cnt >= n_sel, cand, thr)

    thr = lax.fori_loop(0, 31, bit_step, thr)
    cnt_ge = count(lambda blk, idx, t: blk >= t, thr)
    tie_rows = jnp.where(thr > KEY_NEG_INF, cnt_ge, 0) > n_sel
    any_tie = jnp.max(jnp.where(tie_rows, 1, 0)) > 0

    def tie_cut():
        need = n_sel - count(lambda blk, idx, t: blk > t, thr)
        pos = jnp.zeros((qb, 1), I32)
        for b in range(idx_bits - 1, -1, -1):
            cand = pos + (1 << b)
            f = count(lambda blk, idx, t, c: jnp.where(blk == t, idx, lp) < c, thr, cand)
            pos = jnp.where(f < need, cand, pos)
        return pos

    cut = lax.cond(any_tie, tie_cut, lambda: jnp.full((qb, 1), lp, I32))

    m_scr[...] = jnp.full(m_scr.shape, M_INIT, F32)
    acc_scr[...] = jnp.zeros(acc_scr.shape, F32)
    rb = min(qb, 32)

    def p3(j, carry):
        ks = pl.multiple_of(j * tk, tk)
        key = key_scr[:, pl.ds(ks, tk)]
        idx = ks + lane_t
        eq_ok = jnp.where(key == thr, idx, lp + 1) <= cut
        bias = jnp.where(key > thr, 0.0, jnp.where(eq_ok, 0.0, MASK_BIAS))
        bias = jnp.where(key > KEY_NEG_INF, bias, MASK_BIAS)
        if keys_t:
            s_all = [_dot(qg_scr[g], k_ref[g * HEAD_DIM:(g + 1) * HEAD_DIM, pl.ds(ks, tk)])
                     for g in range(DSA_KV)]
        else:
            s_all = [_dot_nt(qg_scr[g], k_ref[pl.ds(ks, tk), g * HEAD_DIM:(g + 1) * HEAD_DIM])
                     for g in range(DSA_KV)]
        for g in range(DSA_KV):
            for r0 in range(0, gq, rb):
                rs = slice(r0, r0 + rb)
                s = s_all[g][rs, :] + bias[r0 % qb:r0 % qb + rb, :]
                m_old = m_scr[g, rs, :]
                m_new = jnp.maximum(m_old, jnp.max(s, axis=1, keepdims=True))
                p_scr[g, rs, :] = jnp.exp2(s - m_new).astype(BF16)
                a_scr[g, rs, :] = jnp.exp2(m_old - m_new)
                m_scr[g, rs, :] = m_new
        for g in range(DSA_KV):
            vt = v_ref[pl.ds(ks, tk), 2 * g * HEAD_DIM:(2 * g + 2) * HEAD_DIM]
            acc_scr[g] = a_scr[g] * acc_scr[g] + _dot(p_scr[g], vt)
        return carry

    lax.fori_loop(0, nt, p3, 0)

    for g in range(DSA_KV):
        acc = acc_scr[g]
        o = acc[:, :HEAD_DIM] / acc[:, HEAD_DIM:]
        for hh in range(DSA_GROUP):
            hd = g * DSA_GROUP + hh
            o_ref[:, hd * HEAD_DIM:(hd + 1) * HEAD_DIM] = o[hh * qb:(hh + 1) * qb, :].astype(BF16)


def dsa_attention(q, qi, proj, k, v, ki, *, row0, nq, qb, tk, n_sel, l_valid, causal_blocks, qpos_static,
                  keys_t=False):
    nb, lp, _ = v.shape
    kvw = DSA_KV * HEAD_DIM
    rb0 = row0 // qb
    gq = DSA_GROUP * qb
    kern = functools.partial(_dsa_kernel, qb=qb, tk=tk, lp=lp, n_sel=n_sel, l_valid=l_valid,
                             causal_blocks=causal_blocks, qpos_static=qpos_static, keys_t=keys_t)
    if keys_t:
        k_spec = pl.BlockSpec((kvw, lp), lambda b, i: (0, b))
        ki_spec = pl.BlockSpec((IDX_DIM, lp), lambda b, i: (0, b))
    else:
        k_spec = pl.BlockSpec((None, lp, kvw), lambda b, i: (b, 0, 0))
        ki_spec = pl.BlockSpec((None, lp, IDX_DIM), lambda b, i: (b, 0, 0))
    return pl.pallas_call(
        kern,
        grid=(nb, nq),
        in_specs=[
            pl.BlockSpec((qb, MIX_B), lambda b, i: (rb0 + b * nq + i, 0)),
            pl.BlockSpec((qb, IDX_HEADS * IDX_DIM), lambda b, i: (rb0 + b * nq + i, 0)),
            pl.BlockSpec((qb, LANES), lambda b, i: (rb0 + b * nq + i, OFF_SMALL // LANES)),
            k_spec,
            pl.BlockSpec((None, lp, 2 * kvw), lambda b, i: (b, 0, 0)),
            ki_spec,
        ],
        out_specs=pl.BlockSpec((qb, MIX_B), lambda b, i: (b * nq + i, 0)),
        out_shape=jax.ShapeDtypeStruct((nb * nq * qb, MIX_B), BF16),
        scratch_shapes=[
            pltpu.VMEM((qb, lp), I32),
            pltpu.VMEM((2, qb, LANES), I32),
            pltpu.VMEM((IDX_HEADS * qb, IDX_DIM), BF16),
            pltpu.VMEM((DSA_KV, gq, HEAD_DIM), BF16),
            pltpu.VMEM((DSA_KV, gq, 1), F32),
            pltpu.VMEM((DSA_KV, gq, 1), F32),
            pltpu.VMEM((DSA_KV, gq, tk), BF16),
            pltpu.VMEM((DSA_KV, gq, 2 * HEAD_DIM), F32),
        ],
        compiler_params=_cparams("parallel", "arbitrary"),
        name="dsa_attention",
    )(q, qi, proj, k, v, ki)


def permute_w_in(w_in):
    offs, o = {}, 0
    for name, size in COL_SIZES:
        offs[name] = (o, size)
        o += size
    parts = [w_in[:, offs[n][0]:offs[n][0] + offs[n][1]] for n in PERM_ORDER]
    used = sum(offs[n][1] for n in PERM_ORDER)
    parts.append(jnp.zeros((w_in.shape[0], P_COLS - used), w_in.dtype))
    return jnp.concatenate(parts, axis=1)


TM_BIG = 1536
TM_DOWN = 768
TC_GLA = 256
TK_DSA = 512
QB_DSA = 256
SAMPLE_LP = 2176
TM_MOE = 512
TR_MOE = 256
MOE_ROWS = -(-(TOP_K * N_TOK + N_EXPERTS * (TM_MOE - 1)) // TM_MOE) * TM_MOE


def kernel(x_prompt, x_sample, cache_k, cache_v, cache_ki, state_gla, g_attn, w_in, w_gla_gate2, b_gla_gate,
           g_gla_out, g_q, g_k, g_ki, w_branch_a, w_branch_b, w_out, g_ffn, w_ff_gate, w_ff_up, w_ff_down,
           w_router, w_moe_gate, w_moe_up, w_moe_down):
    h = jnp.concatenate([x_prompt.reshape(N_PROMPT, D_MODEL), x_sample.reshape(N_SAMPLE, D_MODEL)], axis=0)
    pos = jnp.concatenate([jnp.tile(jnp.arange(SEQ, dtype=I32), BATCH),
                           jnp.tile(PAST_LEN + jnp.arange(DEC_SEQ, dtype=I32), DEC_BATCH)])
    tabs = rope_tables(pos)
    kvw = DSA_KV * HEAD_DIM
    l_sample = PAST_LEN + DEC_SEQ
    zeros_state = jnp.zeros((BATCH, GLA_HEADS, GLA_DK, GLA_DV), F32)
    outs = {n: [] for n in ('kp', 'vp', 'kip', 'sp', 'ks', 'vs', 'kis', 'ss')}

    for l in range(DEPTH):
        hn = rmsnorm_cast(h, g_attn[l], TM_DOWN)
        proj = matmul(hn, permute_w_in(w_in[l]), TM_BIG, 512, name="in_proj")
        q_b, k_f, k_b, v_b, qi_b, ki_f, ki_b, k_t, ki_t = dsa_prep(proj, tabs, g_q[l], g_k[l], g_ki[l], 512)

        wg2 = gla_gate_weight(w_gla_gate2[l])
        oa_p, s_p = gla_scan(proj, wg2, b_gla_gate[l], g_gla_out[l], zeros_state, 0, SEQ, CHUNK, TC_GLA)
        oa_s, s_s = gla_scan(proj, wg2, b_gla_gate[l], g_gla_out[l], state_gla[l], N_PROMPT, DEC_SEQ,
                             DEC_SEQ, DEC_SEQ)

        ob_p = dsa_attention(
            q_b, qi_b, proj,
            k_t, v_b[:N_PROMPT].reshape(BATCH, SEQ, 2 * kvw), ki_t,
            row0=0, nq=SEQ // QB_DSA, qb=QB_DSA, tk=TK_DSA, n_sel=min(TOPK_MAX, SEQ // 4), l_valid=SEQ,
            causal_blocks=True, qpos_static=0, keys_t=True)

        def with_cache(cache_parts, new, width):
            c = jnp.concatenate([p.astype(BF16) for p in cache_parts], axis=-1)
            n = new[N_PROMPT:].reshape(DEC_BATCH, DEC_SEQ, width)
            pad = jnp.zeros((DEC_BATCH, SAMPLE_LP - l_sample, width), BF16)
            return jnp.concatenate([c, n, pad], axis=1)

        ck, cv = cache_k[l], cache_v[l]
        ones = jnp.ones((DEC_BATCH, PAST_LEN, HEAD_DIM), BF16)
        ob_s = dsa_attention(
            q_b, qi_b, proj,
            with_cache([ck[:, :, g, :] for g in range(DSA_KV)], k_b, kvw),
            with_cache([p for g in range(DSA_KV) for p in (cv[:, :, g, :], ones)], v_b, 2 * kvw),
            with_cache([cache_ki[l]], ki_b, IDX_DIM),
            row0=N_PROMPT, nq=1, qb=DEC_SEQ, tk=SAMPLE_LP, n_sel=min(TOPK_MAX, l_sample // 4),
            l_valid=l_sample, causal_blocks=False, qpos_static=PAST_LEN)

        o_a = jnp.concatenate([oa_p, oa_s], axis=0)
        o_b = jnp.concatenate([ob_p, ob_s], axis=0)
        merged = merge_branches(o_a, o_b, w_branch_a[l], w_branch_b[l], proj, TM_BIG, 512)
        h = matmul(merged, w_out[l], TM_BIG, 512, residual=h, name="out_proj")

        j = l // 2
        if l % 2 == 0:
            hn = rmsnorm_cast(h, g_ffn[l], TM_DOWN)
            a = swiglu_up(hn, w_ff_gate[j], w_ff_up[j], TM_BIG, 256)
            h = matmul(a, w_ff_down[j], TM_DOWN, 512, residual=h, name="ffn_down", weight_resident=True)
        else:
            hn, hn_packed = rmsnorm_pack(h, g_ffn[l], TM_DOWN)
            route = moe_router(hn, w_router[j], TM_DOWN)
            dest, tile_expert, n_valid = moe_plan(route, TM_MOE, MOE_ROWS)
            xs = moe_dispatch(hn_packed, dest, MOE_ROWS, TR_MOE)
            a = moe_up(xs, w_moe_gate[j], w_moe_up[j], tile_expert, n_valid, TM_MOE, 512)
            y = moe_down(a, w_moe_down[j], tile_expert, n_valid, TM_MOE, 512)
            h = moe_combine(y, dest, route, h, TR_MOE)

        v_f = proj[:, OFF_DSA_V:OFF_DSA_V + kvw]
        outs['kp'].append(k_f[:N_PROMPT].reshape(BATCH, SEQ, DSA_KV, HEAD_DIM))
        outs['vp'].append(v_f[:N_PROMPT].reshape(BATCH, SEQ, DSA_KV, HEAD_DIM))
        outs['kip'].append(ki_f[:N_PROMPT].reshape(BATCH, SEQ, IDX_DIM))
        outs['sp'].append(s_p)
        outs['ks'].append(k_f[N_PROMPT:].reshape(DEC_BATCH, DEC_SEQ, DSA_KV, HEAD_DIM))
        outs['vs'].append(v_f[N_PROMPT:].reshape(DEC_BATCH, DEC_SEQ, DSA_KV, HEAD_DIM))
        outs['kis'].append(ki_f[N_PROMPT:].reshape(DEC_BATCH, DEC_SEQ, IDX_DIM))
        outs['ss'].append(s_s)

    st = {n: jnp.stack(v) for n, v in outs.items()}
    return (h[:N_PROMPT].reshape(BATCH, SEQ, D_MODEL), h[N_PROMPT:].reshape(DEC_BATCH, DEC_SEQ, D_MODEL),
            st['kp'], st['vp'], st['kip'], st['sp'], st['ks'], st['vs'], st['kis'], st['ss'])
```

```python
import functools

import jax
import jax.numpy as jnp
from jax import lax
from jax.experimental import pallas as pl
from jax.experimental.pallas import tpu as pltpu

F32 = jnp.float32
BF16 = jnp.bfloat16
I32 = jnp.int32

D_MODEL = 2048
BATCH = 2
SEQ = 8192
DEPTH = 2
DEC_BATCH = 32
DEC_SEQ = 16
PAST_LEN = 2048
CHUNK = 64
QBLK = 128
ROPE_THETA = 10000.0
EPS = 1e-6
GLA_HEADS = 4
GLA_DK = 128
GLA_DV = 256
GLA_RANK = 16
GLA_TAU = 16.0
DSA_HEADS = 8
DSA_KV = 2
DSA_GROUP = DSA_HEADS // DSA_KV
HEAD_DIM = 128
IDX_HEADS = 8
IDX_DIM = 64
TOPK_MAX = 256
MIX_A = GLA_HEADS * GLA_DV
MIX_B = DSA_HEADS * HEAD_DIM
D_FF = 5632
N_EXPERTS = 8
TOP_K = 2

N_PROMPT = BATCH * SEQ
N_SAMPLE = DEC_BATCH * DEC_SEQ
N_TOK = N_PROMPT + N_SAMPLE

LANES = 128
VMEM_LIMIT_BYTES = 56 * 1024 * 1024

COL_SIZES = (
    ('gla_q', 512), ('gla_k', 512), ('gla_v', 1024), ('gla_glr', 16), ('gla_r', 1024),
    ('dsa_q', 1024), ('dsa_k', 256), ('dsa_v', 256), ('idx_q', 512), ('idx_k', 64),
    ('idx_w', 8), ('gate_a', 2048), ('gate_b', 2048),
)
PERM_ORDER = ('gla_q', 'gla_k', 'gla_v', 'gla_r', 'dsa_q', 'dsa_k', 'dsa_v', 'idx_q',
              'gate_a', 'gate_b', 'idx_k', 'gla_glr', 'idx_w')
P_COLS = 9728
OFF_GLA_Q, OFF_GLA_K, OFF_GLA_V, OFF_GLA_R = 0, 512, 1024, 2048
OFF_DSA_Q, OFF_DSA_K, OFF_DSA_V, OFF_IDX_Q = 3072, 4096, 4352, 4608
OFF_GATE_A, OFF_GATE_B, OFF_SMALL = 5120, 7168, 9216
SM_IDX_K, SM_GLR, SM_IDX_W = 0, 64, 80

ROUTE_E1, ROUTE_E2, ROUTE_W1, ROUTE_W2 = 0, 1, 2, 3
HI16 = -65536
KEY_NEG_INF = -2139095041
INT_MIN = -2147483648
MASK_BIAS = -1e30
M_INIT = -5e29
Q_SCALE = (HEAD_DIM ** -0.5) * 1.4426950408889634


def _cparams(*sem):
    return pltpu.CompilerParams(dimension_semantics=sem, vmem_limit_bytes=VMEM_LIMIT_BYTES)


def _dot(a, b):
    return jnp.dot(a, b, preferred_element_type=F32)


def _dot_nt(a, b):
    return lax.dot_general(a, b, (((1,), (1,)), ((), ())), preferred_element_type=F32)


def _dot_tn(a, b):
    return lax.dot_general(a, b, (((0,), (0,)), ((), ())), preferred_element_type=F32)


def _split3(a):
    a1 = a.astype(BF16)
    r1 = a - a1.astype(F32)
    a2 = r1.astype(BF16)
    a3 = (r1 - a2.astype(F32)).astype(BF16)
    return a1, a2, a3


def _rmsnorm_kernel(x_ref, g_ref, o_ref):
    x = x_ref[...]
    ms = jnp.mean(x * x, axis=-1, keepdims=True)
    o_ref[...] = ((x * lax.rsqrt(ms + EPS)) * g_ref[...]).astype(BF16)


def rmsnorm_cast(x, g, tm):
    n, d = x.shape
    return pl.pallas_call(
        _rmsnorm_kernel,
        grid=(n // tm,),
        in_specs=[pl.BlockSpec((tm, d), lambda i: (i, 0)), pl.BlockSpec((1, d), lambda i: (0, 0))],
        out_specs=pl.BlockSpec((tm, d), lambda i: (i, 0)),
        out_shape=jax.ShapeDtypeStruct((n, d), BF16),
        compiler_params=_cparams("parallel"),
        name="rmsnorm_cast",
    )(x, g.reshape(1, d))


def _mm_kernel(x_ref, w_ref, o_ref):
    o_ref[...] = _dot(x_ref[...], w_ref[...].astype(BF16))


def _mm_res_kernel(x_ref, w_ref, r_ref, o_ref):
    o_ref[...] = r_ref[...] + _dot(x_ref[...], w_ref[...].astype(BF16))


def matmul(x, w, tm, tn, residual=None, name="matmul", weight_resident=False):
    m, k = x.shape
    n = w.shape[1]
    if weight_resident:
        grid = (n // tn, m // tm)
        rc = lambda a, b: (b, a)
    else:
        grid = (m // tm, n // tn)
        rc = lambda a, b: (a, b)
    in_specs = [pl.BlockSpec((tm, k), lambda a, b: (rc(a, b)[0], 0)),
                pl.BlockSpec((k, tn), lambda a, b: (0, rc(a, b)[1]))]
    args = [x, w]
    body = _mm_kernel
    if residual is not None:
        in_specs.append(pl.BlockSpec((tm, tn), lambda a, b: rc(a, b)))
        args.append(residual)
        body = _mm_res_kernel
    return pl.pallas_call(
        body,
        grid=grid,
        in_specs=in_specs,
        out_specs=pl.BlockSpec((tm, tn), lambda a, b: rc(a, b)),
        out_shape=jax.ShapeDtypeStruct((m, n), F32),
        compiler_params=_cparams("parallel", "parallel"),
        name=name,
    )(*args)


def _merge_kernel(oa_ref, ob_ref, wa_ref, wb_ref, ga_ref, gb_ref, o_ref):
    a = _dot(oa_ref[...], wa_ref[...].astype(BF16))
    b = _dot(ob_ref[...], wb_ref[...].astype(BF16))
    o_ref[...] = (jax.nn.sigmoid(ga_ref[...]) * a + jax.nn.sigmoid(gb_ref[...]) * b).astype(BF16)


def merge_branches(o_a, o_b, w_pa, w_pb, proj, tm, tn):
    m = o_a.shape[0]
    n = w_pa.shape[1]
    ja, jb = OFF_GATE_A // tn, OFF_GATE_B // tn
    return pl.pallas_call(
        _merge_kernel,
        grid=(m // tm, n // tn),
        in_specs=[
            pl.BlockSpec((tm, MIX_A), lambda i, j: (i, 0)),
            pl.BlockSpec((tm, MIX_B), lambda i, j: (i, 0)),
            pl.BlockSpec((MIX_A, tn), lambda i, j: (0, j)),
            pl.BlockSpec((MIX_B, tn), lambda i, j: (0, j)),
            pl.BlockSpec((tm, tn), lambda i, j: (i, ja + j)),
            pl.BlockSpec((tm, tn), lambda i, j: (i, jb + j)),
        ],
        out_specs=pl.BlockSpec((tm, tn), lambda i, j: (i, j)),
        out_shape=jax.ShapeDtypeStruct((m, n), BF16),
        compiler_params=_cparams("parallel", "parallel"),
        name="merge_branches",
    )(o_a, o_b, w_pa, w_pb, proj, proj)


def _swiglu_kernel(x_ref, wg_ref, wu_ref, o_ref):
    x = x_ref[...]
    a = _dot(x, wg_ref[...].astype(BF16))
    b = _dot(x, wu_ref[...].astype(BF16))
    o_ref[...] = (a * jax.nn.sigmoid(a) * b).astype(BF16)


def swiglu_up(x, w_gate, w_up, tm, tn):
    m, k = x.shape
    f = w_gate.shape[1]
    return pl.pallas_call(
        _swiglu_kernel,
        grid=(m // tm, f // tn),
        in_specs=[
            pl.BlockSpec((tm, k), lambda i, j: (i, 0)),
            pl.BlockSpec((k, tn), lambda i, j: (0, j)),
            pl.BlockSpec((k, tn), lambda i, j: (0, j)),
        ],
        out_specs=pl.BlockSpec((tm, tn), lambda i, j: (i, j)),
        out_shape=jax.ShapeDtypeStruct((m, f), BF16),
        compiler_params=_cparams("parallel", "parallel"),
        name="swiglu_up",
    )(x, w_gate, w_up)


def _router_kernel(x_ref, w_ref, o_ref):
    logits = _dot(x_ref[...], w_ref[...].astype(BF16))
    lane = lax.broadcasted_iota(I32, logits.shape, 1)
    logits = jnp.where(lane < N_EXPERTS, logits, -jnp.inf)
    t1 = jnp.max(logits, axis=-1, keepdims=True)
    i1 = jnp.min(jnp.where(logits == t1, lane, LANES), axis=-1, keepdims=True)
    rest = jnp.where(lane == i1, -jnp.inf, logits)
    t2 = jnp.max(rest, axis=-1, keepdims=True)
    i2 = jnp.min(jnp.where(rest == t2, lane, LANES), axis=-1, keepdims=True)
    e2 = jnp.exp(t2 - t1)
    den = 1.0 + e2
    o_ref[...] = jnp.where(lane == ROUTE_E1, i1.astype(F32), 0.0) + jnp.where(lane == ROUTE_E2, i2.astype(F32), 0.0) \
        + jnp.where(lane == ROUTE_W1, 1.0 / den, 0.0) + jnp.where(lane == ROUTE_W2, e2 / den, 0.0)


def moe_router(x, w_router, tm):
    m, k = x.shape
    w_pad = jnp.pad(w_router, ((0, 0), (0, LANES - w_router.shape[1])))
    return pl.pallas_call(
        _router_kernel,
        grid=(m // tm,),
        in_specs=[pl.BlockSpec((tm, k), lambda i: (i, 0)), pl.BlockSpec((k, LANES), lambda i: (0, 0))],
        out_specs=pl.BlockSpec((tm, LANES), lambda i: (i, 0)),
        out_shape=jax.ShapeDtypeStruct((m, LANES), F32),
        compiler_params=_cparams("parallel"),
        name="moe_router",
    )(x, w_pad)


def _rmsnorm_pack_kernel(x_ref, g_ref, o_ref, p_ref):
    x = x_ref[...]
    ms = jnp.mean(x * x, axis=-1, keepdims=True)
    y = ((x * lax.rsqrt(ms + EPS)) * g_ref[...]).astype(BF16)
    o_ref[...] = y
    bits = pltpu.bitcast(y.astype(F32), I32)
    half = x.shape[1] // 2
    p_ref[...] = (bits[:, half:] & HI16) | lax.shift_right_logical(bits[:, :half], 16)


def rmsnorm_pack(x, g, tm):
    n, d = x.shape
    return pl.pallas_call(
        _rmsnorm_pack_kernel,
        grid=(n // tm,),
        in_specs=[pl.BlockSpec((tm, d), lambda i: (i, 0)), pl.BlockSpec((1, d), lambda i: (0, 0))],
        out_specs=[pl.BlockSpec((tm, d), lambda i: (i, 0)), pl.BlockSpec((tm, d // 2), lambda i: (i, 0))],
        out_shape=[jax.ShapeDtypeStruct((n, d), BF16), jax.ShapeDtypeStruct((n, d // 2), I32)],
        compiler_params=_cparams("parallel"),
        name="rmsnorm_pack",
    )(x, g.reshape(1, d))


def _unpack_bf16(p):
    lo = pltpu.bitcast(lax.shift_left(p, 16), F32).astype(BF16)
    hi = pltpu.bitcast(p & HI16, F32).astype(BF16)
    return lo, hi


def moe_plan(route, tm, m_pad):
    n = route.shape[0]
    ef = route[:, ROUTE_E1:ROUTE_E2 + 1].astype(I32).reshape(2 * n)
    onehot = (ef[:, None] == jnp.arange(N_EXPERTS, dtype=I32)[None, :]).astype(I32)
    counts = jnp.sum(onehot, axis=0)
    rank = jnp.sum((jnp.cumsum(onehot, axis=0) - onehot) * onehot, axis=1)
    padded = ((counts + tm - 1) // tm) * tm
    ends = jnp.cumsum(padded)
    dest = jnp.sum(onehot * (ends - padded)[None, :], axis=1) + rank
    tile_start = jnp.arange(m_pad // tm, dtype=I32) * tm
    tile_expert = jnp.minimum(jnp.sum((tile_start[:, None] >= ends[None, :]).astype(I32), axis=1), N_EXPERTS - 1)
    n_valid = (ends[-1] // tm).reshape(1)
    return dest, tile_expert, n_valid


def _dispatch_kernel(dest_ref, x_ref, xs_in_ref, xs_ref, sem):
    del xs_in_ref
    rows = x_ref.shape[0]

    def row_copy(r, s):
        d = dest_ref[0, 0, TOP_K * r + s]
        return pltpu.make_async_copy(x_ref.at[pl.ds(r, 1), :], xs_ref.at[pl.ds(d, 1), :], sem)

    def start(r, c):
        for s in range(TOP_K):
            row_copy(r, s).start()
        return c

    def wait(r, c):
        for s in range(TOP_K):
            row_copy(r, s).wait()
        return c

    lax.fori_loop(0, rows, start, 0)
    lax.fori_loop(0, rows, wait, 0)


def moe_dispatch(xp, dest, m_pad, tr):
    n, w = xp.shape
    return pl.pallas_call(
        _dispatch_kernel,
        grid=(n // tr,),
        in_specs=[
            pl.BlockSpec((1, 1, TOP_K * tr), lambda i: (i, 0, 0), memory_space=pltpu.SMEM),
            pl.BlockSpec((tr, w), lambda i: (i, 0)),
            pl.BlockSpec(memory_space=pl.ANY),
        ],
        out_specs=pl.BlockSpec(memory_space=pl.ANY),
        out_shape=jax.ShapeDtypeStruct((m_pad, w), I32),
        scratch_shapes=[pltpu.SemaphoreType.DMA(())],
        input_output_aliases={2: 0},
        compiler_params=_cparams("arbitrary"),
        name="moe_dispatch",
    )(dest.reshape(n // tr, 1, TOP_K * tr), xp, jnp.zeros((m_pad, w), I32))


def _moe_up_kernel(te_ref, nv_ref, x_ref, wg_ref, wu_ref, o_ref):
    del te_ref
    valid = pl.program_id(1) < nv_ref[0]

    @pl.when(valid)
    def _():
        lo, hi = _unpack_bf16(x_ref[...])
        half = x_ref.shape[1]
        a = _dot(lo, wg_ref[:half, :].astype(BF16)) + _dot(hi, wg_ref[half:, :].astype(BF16))
        b = _dot(lo, wu_ref[:half, :].astype(BF16)) + _dot(hi, wu_ref[half:, :].astype(BF16))
        o_ref[...] = (a * jax.nn.sigmoid(a) * b).astype(BF16)

    @pl.when(jnp.logical_not(valid))
    def _():
        o_ref[...] = jnp.zeros_like(o_ref)


def moe_up(xs, w_gate, w_up, tile_expert, n_valid, tm, tn):
    m = xs.shape[0]
    _, k, f = w_gate.shape
    grid_spec = pltpu.PrefetchScalarGridSpec(
        num_scalar_prefetch=2,
        grid=(f // tn, m // tm),
        in_specs=[
            pl.BlockSpec((tm, k // 2), lambda j, i, te, nv: (i, 0)),
            pl.BlockSpec((None, k, tn), lambda j, i, te, nv: (te[i], 0, j)),
            pl.BlockSpec((None, k, tn), lambda j, i, te, nv: (te[i], 0, j)),
        ],
        out_specs=pl.BlockSpec((tm, tn), lambda j, i, te, nv: (i, j)),
    )
    return pl.pallas_call(
        _moe_up_kernel,
        grid_spec=grid_spec,
        out_shape=jax.ShapeDtypeStruct((m, f), BF16),
        compiler_params=_cparams("parallel", "arbitrary"),
        name="moe_up",
    )(tile_expert, n_valid, xs, w_gate, w_up)


def _moe_down_kernel(te_ref, nv_ref, a_ref, w_ref, o_ref):
    del te_ref
    valid = pl.program_id(1) < nv_ref[0]

    @pl.when(valid)
    def _():
        o_ref[...] = _dot(a_ref[...], w_ref[...].astype(BF16))

    @pl.when(jnp.logical_not(valid))
    def _():
        o_ref[...] = jnp.zeros_like(o_ref)


def moe_down(a, w_down, tile_expert, n_valid, tm, tn):
    m, f = a.shape
    d = w_down.shape[2]
    grid_spec = pltpu.PrefetchScalarGridSpec(
        num_scalar_prefetch=2,
        grid=(d // tn, m // tm),
        in_specs=[
            pl.BlockSpec((tm, f), lambda j, i, te, nv: (i, 0)),
            pl.BlockSpec((None, f, tn), lambda j, i, te, nv: (te[i], 0, j)),
        ],
        out_specs=pl.BlockSpec((tm, tn), lambda j, i, te, nv: (i, j)),
    )
    return pl.pallas_call(
        _moe_down_kernel,
        grid_spec=grid_spec,
        out_shape=jax.ShapeDtypeStruct((m, d), F32),
        compiler_params=_cparams("parallel", "arbitrary"),
        name="moe_down",
    )(tile_expert, n_valid, a, w_down)


def _combine_kernel(dest_ref, y_ref, h_ref, rt_ref, o_ref, buf_ref, sem):
    rows = h_ref.shape[0]

    def row_copy(r, s):
        d = dest_ref[0, 0, TOP_K * r + s]
        return pltpu.make_async_copy(y_ref.at[pl.ds(d, 1), :], buf_ref.at[s, pl.ds(r, 1), :], sem.at[s])

    def start(r, c):
        for s in range(TOP_K):
            row_copy(r, s).start()
        return c

    def wait(r, c):
        for s in range(TOP_K):
            row_copy(r, s).wait()
        return c

    lax.fori_loop(0, rows, start, 0)
    lax.fori_loop(0, rows, wait, 0)
    rt = rt_ref[...]
    o_ref[...] = h_ref[...] + (rt[:, ROUTE_W1:ROUTE_W1 + 1] * buf_ref[0] + rt[:, ROUTE_W2:ROUTE_W2 + 1] * buf_ref[1])


def moe_combine(y, dest, route, h, tr):
    n, d = h.shape
    return pl.pallas_call(
        _combine_kernel,
        grid=(n // tr,),
        in_specs=[
            pl.BlockSpec((1, 1, TOP_K * tr), lambda i: (i, 0, 0), memory_space=pltpu.SMEM),
            pl.BlockSpec(memory_space=pl.ANY),
            pl.BlockSpec((tr, d), lambda i: (i, 0)),
            pl.BlockSpec((tr, LANES), lambda i: (i, 0)),
        ],
        out_specs=pl.BlockSpec((tr, d), lambda i: (i, 0)),
        out_shape=jax.ShapeDtypeStruct((n, d), F32),
        scratch_shapes=[pltpu.VMEM((TOP_K, tr, d), F32), pltpu.SemaphoreType.DMA((TOP_K,))],
        compiler_params=_cparams("arbitrary"),
        name="moe_combine",
    )(dest.reshape(n // tr, 1, TOP_K * tr), y, h, route)


def _prep_kernel(q_ref, k_ref, v_ref, qi_ref, sm_ref, c128_ref, s128_ref, c64_ref, s64_ref,
                 gq_ref, gk_ref, gki_ref,
                 qo_ref, kf_ref, kb_ref, vb_ref, qio_ref, kif_ref, kib_ref, kt_ref, kit_ref):
    c128, s128 = c128_ref[...], s128_ref[...]
    c64, s64 = c64_ref[...], s64_ref[...]
    lane = lax.broadcasted_iota(I32, c64.shape, 1)
    lane_lo = (lane & 63) < 32

    def rope128(x):
        return x * c128 + pltpu.roll(x, 64, 1) * s128

    def rope64(x):
        partner = jnp.where(lane_lo, pltpu.roll(x, 96, 1), pltpu.roll(x, 32, 1))
        return x * c64 + partner * s64

    def norm128(x, g):
        ms = jnp.mean(x * x, axis=-1, keepdims=True)
        return x * lax.rsqrt(ms + EPS) * g

    gq, gk = gq_ref[...], gk_ref[...]
    for h in range(DSA_HEADS):
        sl = slice(h * HEAD_DIM, (h + 1) * HEAD_DIM)
        qo_ref[:, sl] = (rope128(norm128(q_ref[:, sl], gq)) * Q_SCALE).astype(BF16)
    for h in range(DSA_KV):
        sl = slice(h * HEAD_DIM, (h + 1) * HEAD_DIM)
        kr = rope128(norm128(k_ref[:, sl], gk))
        kf_ref[:, sl] = kr
        kb_ref[:, sl] = kr.astype(BF16)
        kt_ref[sl, :] = kr.T.astype(BF16)
    ones = jnp.ones((v_ref.shape[0], HEAD_DIM), BF16)
    for h in range(DSA_KV):
        vb_ref[:, 2 * h * HEAD_DIM:(2 * h + 1) * HEAD_DIM] = v_ref[:, h * HEAD_DIM:(h + 1) * HEAD_DIM].astype(BF16)
        vb_ref[:, (2 * h + 1) * HEAD_DIM:(2 * h + 2) * HEAD_DIM] = ones
    for t in range(IDX_HEADS * IDX_DIM // LANES):
        sl = slice(t * LANES, (t + 1) * LANES)
        qio_ref[:, sl] = rope64(qi_ref[:, sl]).astype(BF16)
    sm = sm_ref[...]
    ms = jnp.sum(jnp.where(lane < IDX_DIM, sm * sm, 0.0), axis=-1, keepdims=True) * (1.0 / IDX_DIM)
    ki = rope64(sm * lax.rsqrt(ms + EPS) * gki_ref[...])
    kif_ref[...] = ki[:, :IDX_DIM]
    kib_ref[...] = ki[:, :IDX_DIM].astype(BF16)
    kit_ref[...] = ki.T[:IDX_DIM, :].astype(BF16)


def dsa_prep(proj, tabs, g_q, g_k, g_ki, tm):
    n = proj.shape[0]
    c128, s128, c64, s64 = tabs
    gki_pad = jnp.pad(g_ki, (0, LANES - IDX_DIM)).reshape(1, LANES)
    row = lambda w, off: pl.BlockSpec((tm, w), lambda i: (i, off // w))
    tab = pl.BlockSpec((tm, LANES), lambda i: (i, 0))
    vec = pl.BlockSpec((1, LANES), lambda i: (0, 0))
    out = lambda w: pl.BlockSpec((tm, w), lambda i: (i, 0))
    kvw = DSA_KV * HEAD_DIM
    return pl.pallas_call(
        _prep_kernel,
        grid=(n // tm,),
        in_specs=[row(MIX_B, OFF_DSA_Q), row(kvw, OFF_DSA_K), row(kvw, OFF_DSA_V),
                  row(IDX_HEADS * IDX_DIM, OFF_IDX_Q), row(LANES, OFF_SMALL),
                  tab, tab, tab, tab, vec, vec, vec],
        out_specs=[out(MIX_B), out(kvw), out(kvw), out(2 * kvw), out(IDX_HEADS * IDX_DIM),
                   out(IDX_DIM), out(IDX_DIM),
                   pl.BlockSpec((kvw, tm), lambda i: (0, i)), pl.BlockSpec((IDX_DIM, tm), lambda i: (0, i))],
        out_shape=[
            jax.ShapeDtypeStruct((n, MIX_B), BF16),
            jax.ShapeDtypeStruct((n, kvw), F32),
            jax.ShapeDtypeStruct((n, kvw), BF16),
            jax.ShapeDtypeStruct((n, 2 * kvw), BF16),
            jax.ShapeDtypeStruct((n, IDX_HEADS * IDX_DIM), BF16),
            jax.ShapeDtypeStruct((n, IDX_DIM), F32),
            jax.ShapeDtypeStruct((n, IDX_DIM), BF16),
            jax.ShapeDtypeStruct((kvw, n), BF16),
            jax.ShapeDtypeStruct((IDX_DIM, n), BF16),
        ],
        compiler_params=_cparams("parallel"),
        name="dsa_prep",
    )(proj, proj, proj, proj, proj, c128, s128, c64, s64,
      g_q.reshape(1, LANES), g_k.reshape(1, LANES), gki_pad)


def rope_tables(pos):
    def tab(half, reps):
        inv = ROPE_THETA ** (-jnp.arange(half, dtype=F32) / half)
        ang = pos.astype(F32)[:, None] * inv[None, :]
        cos, sin = jnp.cos(ang), jnp.sin(ang)
        return jnp.tile(jnp.concatenate([cos, cos], -1), (1, reps)), jnp.tile(jnp.concatenate([-sin, sin], -1), (1, reps))
    c128, s128 = tab(HEAD_DIM // 2, 1)
    c64, s64 = tab(IDX_DIM // 2, 2)
    return c128, s128, c64, s64


def _gla_kernel(q_ref, k_ref, v_ref, r_ref, sm_ref, wg2_ref, bg_ref, gout_ref, s0_ref,
                o_ref, sout_ref, st_ref, *, chunk, n_chunks):
    t = pl.program_id(1)

    @pl.when(t == 0)
    def _():
        for h in range(GLA_HEADS):
            st_ref[h] = s0_ref[h].T

    row = lax.broadcasted_iota(I32, (chunk, chunk), 0)
    col = lax.broadcasted_iota(I32, (chunk, chunk), 1)
    causal = row >= col
    tril = jnp.where(causal, 1.0, 0.0).astype(BF16)
    gout = gout_ref[...]

    heads = range(GLA_HEADS)
    dks = [slice(h * GLA_DK, (h + 1) * GLA_DK) for h in heads]
    dvs = [slice(h * GLA_DV, (h + 1) * GLA_DV) for h in heads]
    ws = [_split3(wg2_ref[h]) for h in heads]
    bgs = [bg_ref[:, dks[h]] for h in heads]
    sts = [st_ref[h] for h in heads]
    for c in range(n_chunks):
        sl = pl.ds(c * chunk, chunk)
        s1, s2, s3 = _split3(sm_ref[sl, :])
        qs = [q_ref[sl, dks[h]] * (GLA_DK ** -0.5) for h in heads]
        ks = [k_ref[sl, dks[h]] for h in heads]
        vs = [v_ref[sl, dvs[h]].astype(BF16) for h in heads]
        rs = [r_ref[sl, dvs[h]] for h in heads]
        zs = [(_dot(s1, w[0]) + (_dot(s1, w[1]) + _dot(s2, w[0]))
               + (_dot(s1, w[2]) + _dot(s2, w[1]) + _dot(s3, w[0]))) + bgs[h] for h, w in enumerate(ws)]
        gs = [_split3((jnp.minimum(z, 0.0) - jnp.log1p(jnp.exp(-jnp.abs(z)))) * (1.0 / GLA_TAU)) for z in zs]
        bs = [_dot(tril, g[0]) + _dot(tril, g[1]) + _dot(tril, g[2]) for g in gs]
        b_last = [b[chunk - 1:chunk, :] for b in bs]
        qe = [(qs[h] * jnp.exp(bs[h])).astype(BF16) for h in heads]
        ke = [(ks[h] * jnp.exp(-bs[h])).astype(BF16) for h in heads]
        kd = [(ks[h] * jnp.exp(b_last[h] - bs[h])).astype(BF16) for h in heads]
        a = [jnp.where(causal, _dot_nt(qe[h], ke[h]), 0.0).astype(BF16) for h in heads]
        o = [_dot(a[h], vs[h]) + _dot_nt(qe[h], sts[h].astype(BF16)) for h in heads]
        sts = [jnp.exp(b_last[h]) * sts[h] + _dot_tn(vs[h], kd[h]) for h in heads]
        for h in heads:
            ms = jnp.mean(o[h] * o[h], axis=-1, keepdims=True)
            on = o[h] * lax.rsqrt(ms + EPS) * gout
            o_ref[sl, dvs[h]] = (rs[h] * jax.nn.sigmoid(rs[h]) * on).astype(BF16)
    for h in heads:
        st_ref[h] = sts[h]

    @pl.when(t == pl.num_programs(1) - 1)
    def _():
        for h in range(GLA_HEADS):
            sout_ref[h] = st_ref[h].T


def gla_scan(proj, wg2_pad, b_g, g_out, s0, row0, seq, chunk, tc):
    nb = s0.shape[0]
    nt = seq // tc
    rb0 = row0 // tc
    hk, hv = GLA_HEADS * GLA_DK, GLA_HEADS * GLA_DV
    rows = lambda w, off: pl.BlockSpec((tc, w), lambda b, t: (rb0 + b * nt + t, off // w))
    state = pl.BlockSpec((None, GLA_HEADS, GLA_DK, GLA_DV), lambda b, t: (b, 0, 0, 0))
    kern = functools.partial(_gla_kernel, chunk=chunk, n_chunks=tc // chunk)
    return pl.pallas_call(
        kern,
        grid=(nb, nt),
        in_specs=[
            rows(hk, OFF_GLA_Q), rows(hk, OFF_GLA_K), rows(hv, OFF_GLA_V), rows(hv, OFF_GLA_R),
            rows(LANES, OFF_SMALL),
            pl.BlockSpec((GLA_HEADS, LANES, GLA_DK), lambda b, t: (0, 0, 0)),
            pl.BlockSpec((1, hk), lambda b, t: (0, 0)),
            pl.BlockSpec((1, GLA_DV), lambda b, t: (0, 0)),
            state,
        ],
        out_specs=[pl.BlockSpec((tc, hv), lambda b, t: (b * nt + t, 0)), state],
        out_shape=[
            jax.ShapeDtypeStruct((nb * seq, MIX_A), BF16),
            jax.ShapeDtypeStruct((nb, GLA_HEADS, GLA_DK, GLA_DV), F32),
        ],
        scratch_shapes=[pltpu.VMEM((GLA_HEADS, GLA_DV, GLA_DK), F32)],
        compiler_params=_cparams("parallel", "arbitrary"),
        name="gla_scan",
    )(proj, proj, proj, proj, proj, wg2_pad, b_g.reshape(1, -1), g_out.reshape(1, -1), s0)


def gla_gate_weight(w_g2):
    w = w_g2.reshape(GLA_RANK, GLA_HEADS, GLA_DK).transpose(1, 0, 2)
    return jnp.pad(w, ((0, 0), (SM_GLR, LANES - SM_GLR - GLA_RANK), (0, 0)))


def _dsa_kernel(q_ref, qi_ref, sm_ref, k_ref, v_ref, ki_ref, o_ref,
                score_scr, vec_scr, qis_scr, qg_scr, m_scr, a_scr, p_scr, acc_scr,
                *, qb, tk, lp, n_sel, l_valid, causal_blocks, qpos_static, keys_t):
    gq = DSA_GROUP * qb
    if causal_blocks:
        i = pl.program_id(1)
        qpos0 = i * qb
        nt = ((i + 1) * qb + tk - 1) // tk
    else:
        qpos0 = qpos_static
        nt = lp // tk
    nsub = tk // LANES
    idx_bits = max(1, (lp - 1).bit_length())

    for h in range(IDX_HEADS):
        qis_scr[h * qb:(h + 1) * qb, :] = qi_ref[:, h * IDX_DIM:(h + 1) * IDX_DIM]
    for g in range(DSA_KV):
        for hh in range(DSA_GROUP):
            hd = g * DSA_GROUP + hh
            qg_scr[g, hh * qb:(hh + 1) * qb, :] = q_ref[:, hd * HEAD_DIM:(hd + 1) * HEAD_DIM]
    wq = sm_ref[:, SM_IDX_W:SM_IDX_W + IDX_HEADS] * ((IDX_DIM ** -0.5) * (IDX_HEADS ** -0.5))
    w_cols = [wq[:, h:h + 1] for h in range(IDX_HEADS)]
    row_chunk = (qpos0 + lax.broadcasted_iota(I32, (qb, tk), 0)) >> 6
    lane_t = lax.broadcasted_iota(I32, (qb, tk), 1)

    def p1(j, carry):
        ks = pl.multiple_of(j * tk, tk)
        if keys_t:
            s_all = _dot(qis_scr[...], ki_ref[:, pl.ds(ks, tk)])
        else:
            s_all = _dot_nt(qis_scr[...], ki_ref[pl.ds(ks, tk), :])
        acc = jnp.zeros((qb, tk), F32)
        for h in range(IDX_HEADS):
            acc = acc + jnp.maximum(s_all[h * qb:(h + 1) * qb, :], 0.0) * w_cols[h]
        kpos = ks + lane_t
        adm = jnp.where((kpos >> 6) <= row_chunk, kpos, l_valid) < l_valid
        score_scr[:, pl.ds(ks, tk)] = jnp.where(adm, acc, -jnp.inf)
        return carry

    lax.fori_loop(0, nt, p1, 0)

    rh = min(qb, 128)

    row_groups = list(range(0, qb, rh))

    def count(pred, *row_vecs):
        for i, v in enumerate(row_vecs):
            vec_scr[i] = jnp.broadcast_to(v, (qb, LANES))

        def body(j, accs):
            accs = list(accs)
            for gi, r0 in enumerate(row_groups):
                vecs = [vec_scr[i, r0:r0 + rh, :] for i in range(len(row_vecs))]
                for c in range(nsub):
                    off = pl.multiple_of(j * tk + c * LANES, LANES)
                    blk = score_scr[r0:r0 + rh, pl.ds(off, LANES)]
                    idx = off + lax.broadcasted_iota(I32, (rh, LANES), 1)
                    accs[gi] = accs[gi] + jnp.where(pred(blk, idx, *vecs), 1, 0)
            return tuple(accs)

        accs = lax.fori_loop(0, nt, body, tuple(jnp.zeros((rh, LANES), I32) for _ in row_groups))
        acc = accs[0] if len(accs) == 1 else jnp.concatenate(accs, axis=0)
        return jnp.sum(acc, axis=1, keepdims=True)

    def key_value(key):
        return pltpu.bitcast(key ^ ((key >> 31) & 0x7FFFFFFF), F32)

    c0 = count(lambda blk, idx: blk >= 0.0)
    thr_key = jnp.where(c0 >= n_sel, 0, INT_MIN).astype(I32)

    def bit_step(it, thr_key):
        cand = thr_key + lax.shift_left(jnp.int32(1), 30 - it)
        cnt = count(lambda blk, idx, c: blk >= c, key_value(cand))
        return jnp.where(cnt >= n_sel, cand, thr_key)

    thr_key = lax.fori_loop(0, 31, bit_step, thr_key)
    bounded = thr_key > KEY_NEG_INF
    thr = jnp.where(bounded, key_value(thr_key), -jnp.inf)
    cnt_ge = count(lambda blk, idx, t: blk >= t, thr)
    tie_rows = jnp.where(bounded, cnt_ge, 0) > n_sel
    any_tie = jnp.max(jnp.where(tie_rows, 1, 0)) > 0

    def tie_cut():
        need = n_sel - count(lambda blk, idx, t: blk > t, thr)
        pos = jnp.zeros((qb, 1), I32)
        for b in range(idx_bits - 1, -1, -1):
            cand = pos + (1 << b)
            f = count(lambda blk, idx, t, c: jnp.where(blk == t, idx.astype(F32), float(lp)) < c,
                      thr, cand.astype(F32))
            pos = jnp.where(f < need, cand, pos)
        return pos

    cut = lax.cond(any_tie, tie_cut, lambda: jnp.full((qb, 1), lp, I32))

    m_scr[...] = jnp.full(m_scr.shape, M_INIT, F32)
    acc_scr[...] = jnp.zeros(acc_scr.shape, F32)
    rb = min(qb, 32)

    def p3(j, carry):
        ks = pl.multiple_of(j * tk, tk)
        score = score_scr[:, pl.ds(ks, tk)]
        idx = ks + lane_t
        eq_ok = jnp.where(score == thr, idx, lp + 1) <= cut
        bias = jnp.where(score > thr, 0.0, jnp.where(eq_ok, 0.0, MASK_BIAS))
        bias = jnp.where(score > -jnp.inf, bias, MASK_BIAS)
        if keys_t:
            s_all = [_dot(qg_scr[g], k_ref[g * HEAD_DIM:(g + 1) * HEAD_DIM, pl.ds(ks, tk)])
                     for g in range(DSA_KV)]
        else:
            s_all = [_dot_nt(qg_scr[g], k_ref[pl.ds(ks, tk), g * HEAD_DIM:(g + 1) * HEAD_DIM])
                     for g in range(DSA_KV)]
        for g in range(DSA_KV):
            for r0 in range(0, gq, rb):
                rs = slice(r0, r0 + rb)
                s = s_all[g][rs, :] + bias[r0 % qb:r0 % qb + rb, :]
                m_old = m_scr[g, rs, :]
                m_new = jnp.maximum(m_old, jnp.max(s, axis=1, keepdims=True))
                p_scr[g, rs, :] = jnp.exp2(s - m_new).astype(BF16)
                a_scr[g, rs, :] = jnp.exp2(m_old - m_new)
                m_scr[g, rs, :] = m_new
        for g in range(DSA_KV):
            vt = v_ref[pl.ds(ks, tk), 2 * g * HEAD_DIM:(2 * g + 2) * HEAD_DIM]
            acc_scr[g] = a_scr[g] * acc_scr[g] + _dot(p_scr[g], vt)
        return carry

    lax.fori_loop(0, nt, p3, 0)

    for g in range(DSA_KV):
        acc = acc_scr[g]
        o = acc[:, :HEAD_DIM] / acc[:, HEAD_DIM:]
        for hh in range(DSA_GROUP):
            hd = g * DSA_GROUP + hh
            o_ref[:, hd * HEAD_DIM:(hd + 1) * HEAD_DIM] = o[hh * qb:(hh + 1) * qb, :].astype(BF16)


def dsa_attention(q, qi, proj, k, v, ki, *, row0, nq, qb, tk, n_sel, l_valid, causal_blocks, qpos_static,
                  keys_t=False):
    nb, lp, _ = v.shape
    kvw = DSA_KV * HEAD_DIM
    rb0 = row0 // qb
    gq = DSA_GROUP * qb
    kern = functools.partial(_dsa_kernel, qb=qb, tk=tk, lp=lp, n_sel=n_sel, l_valid=l_valid,
                             causal_blocks=causal_blocks, qpos_static=qpos_static, keys_t=keys_t)
    if keys_t:
        k_spec = pl.BlockSpec((kvw, lp), lambda b, i: (0, b))
        ki_spec = pl.BlockSpec((IDX_DIM, lp), lambda b, i: (0, b))
    else:
        k_spec = pl.BlockSpec((None, lp, kvw), lambda b, i: (b, 0, 0))
        ki_spec = pl.BlockSpec((None, lp, IDX_DIM), lambda b, i: (b, 0, 0))
    return pl.pallas_call(
        kern,
        grid=(nb, nq),
        in_specs=[
            pl.BlockSpec((qb, MIX_B), lambda b, i: (rb0 + b * nq + i, 0)),
            pl.BlockSpec((qb, IDX_HEADS * IDX_DIM), lambda b, i: (rb0 + b * nq + i, 0)),
            pl.BlockSpec((qb, LANES), lambda b, i: (rb0 + b * nq + i, OFF_SMALL // LANES)),
            k_spec,
            pl.BlockSpec((None, lp, 2 * kvw), lambda b, i: (b, 0, 0)),
            ki_spec,
        ],
        out_specs=pl.BlockSpec((qb, MIX_B), lambda b, i: (b * nq + i, 0)),
        out_shape=jax.ShapeDtypeStruct((nb * nq * qb, MIX_B), BF16),
        scratch_shapes=[
            pltpu.VMEM((qb, lp), F32),
            pltpu.VMEM((2, qb, LANES), F32),
            pltpu.VMEM((IDX_HEADS * qb, IDX_DIM), BF16),
            pltpu.VMEM((DSA_KV, gq, HEAD_DIM), BF16),
            pltpu.VMEM((DSA_KV, gq, 1), F32),
            pltpu.VMEM((DSA_KV, gq, 1), F32),
            pltpu.VMEM((DSA_KV, gq, tk), BF16),
            pltpu.VMEM((DSA_KV, gq, 2 * HEAD_DIM), F32),
        ],
        compiler_params=_cparams("parallel", "arbitrary"),
        name="dsa_attention",
    )(q, qi, proj, k, v, ki)


def permute_w_in(w_in):
    offs, o = {}, 0
    for name, size in COL_SIZES:
        offs[name] = (o, size)
        o += size
    parts = [w_in[:, offs[n][0]:offs[n][0] + offs[n][1]] for n in PERM_ORDER]
    used = sum(offs[n][1] for n in PERM_ORDER)
    parts.append(jnp.zeros((w_in.shape[0], P_COLS - used), w_in.dtype))
    return jnp.concatenate(parts, axis=1)


TM_BIG = 1536
TM_DOWN = 768
TC_GLA = 256
TK_DSA = 512
QB_DSA = 256
SAMPLE_LP = 2176
TM_MOE = 512
TR_MOE = 256
MOE_ROWS = -(-(TOP_K * N_TOK + N_EXPERTS * (TM_MOE - 1)) // TM_MOE) * TM_MOE


def kernel(x_prompt, x_sample, cache_k, cache_v, cache_ki, state_gla, g_attn, w_in, w_gla_gate2, b_gla_gate,
           g_gla_out, g_q, g_k, g_ki, w_branch_a, w_branch_b, w_out, g_ffn, w_ff_gate, w_ff_up, w_ff_down,
           w_router, w_moe_gate, w_moe_up, w_moe_down):
    h = jnp.concatenate([x_prompt.reshape(N_PROMPT, D_MODEL), x_sample.reshape(N_SAMPLE, D_MODEL)], axis=0)
    pos = jnp.concatenate([jnp.tile(jnp.arange(SEQ, dtype=I32), BATCH),
                           jnp.tile(PAST_LEN + jnp.arange(DEC_SEQ, dtype=I32), DEC_BATCH)])
    tabs = rope_tables(pos)
    kvw = DSA_KV * HEAD_DIM
    l_sample = PAST_LEN + DEC_SEQ
    zeros_state = jnp.zeros((BATCH, GLA_HEADS, GLA_DK, GLA_DV), F32)
    outs = {n: [] for n in ('kp', 'vp', 'kip', 'sp', 'ks', 'vs', 'kis', 'ss')}

    for l in range(DEPTH):
        hn = rmsnorm_cast(h, g_attn[l], TM_DOWN)
        proj = matmul(hn, permute_w_in(w_in[l]), TM_BIG, 512, name="in_proj")
        q_b, k_f, k_b, v_b, qi_b, ki_f, ki_b, k_t, ki_t = dsa_prep(proj, tabs, g_q[l], g_k[l], g_ki[l], 512)

        wg2 = gla_gate_weight(w_gla_gate2[l])
        oa_p, s_p = gla_scan(proj, wg2, b_gla_gate[l], g_gla_out[l], zeros_state, 0, SEQ, CHUNK, TC_GLA)
        oa_s, s_s = gla_scan(proj, wg2, b_gla_gate[l], g_gla_out[l], state_gla[l], N_PROMPT, DEC_SEQ,
                             DEC_SEQ, DEC_SEQ)

        ob_p = dsa_attention(
            q_b, qi_b, proj,
            k_t, v_b[:N_PROMPT].reshape(BATCH, SEQ, 2 * kvw), ki_t,
            row0=0, nq=SEQ // QB_DSA, qb=QB_DSA, tk=TK_DSA, n_sel=min(TOPK_MAX, SEQ // 4), l_valid=SEQ,
            causal_blocks=True, qpos_static=0, keys_t=True)

        def with_cache(cache_parts, new, width):
            c = jnp.concatenate([p.astype(BF16) for p in cache_parts], axis=-1)
            n = new[N_PROMPT:].reshape(DEC_BATCH, DEC_SEQ, width)
            pad = jnp.zeros((DEC_BATCH, SAMPLE_LP - l_sample, width), BF16)
            return jnp.concatenate([c, n, pad], axis=1)

        ck, cv = cache_k[l], cache_v[l]
        ones = jnp.ones((DEC_BATCH, PAST_LEN, HEAD_DIM), BF16)
        ob_s = dsa_attention(
            q_b, qi_b, proj,
            with_cache([ck[:, :, g, :] for g in range(DSA_KV)], k_b, kvw),
            with_cache([p for g in range(DSA_KV) for p in (cv[:, :, g, :], ones)], v_b, 2 * kvw),
            with_cache([cache_ki[l]], ki_b, IDX_DIM),
            row0=N_PROMPT, nq=1, qb=DEC_SEQ, tk=SAMPLE_LP, n_sel=min(TOPK_MAX, l_sample // 4),
            l_valid=l_sample, causal_blocks=False, qpos_static=PAST_LEN)

        o_a = jnp.concatenate([oa_p, oa_s], axis=0)
        o_b = jnp.concatenate([ob_p, ob_s], axis=0)
        merged = merge_branches(o_a, o_b, w_branch_a[l], w_branch_b[l], proj, TM_BIG, 512)
        h = matmul(merged, w_out[l], TM_BIG, 512, residual=h, name="out_proj")

        j = l // 2
        if l % 2 == 0:
            hn = rmsnorm_cast(h, g_ffn[l], TM_DOWN)
            a = swiglu_up(hn, w_ff_gate[j], w_ff_up[j], TM_BIG, 256)
            h = matmul(a, w_ff_down[j], TM_DOWN, 512, residual=h, name="ffn_down", weight_resident=True)
        else:
            hn, hn_packed = rmsnorm_pack(h, g_ffn[l], TM_DOWN)
            route = moe_router(hn, w_router[j], TM_DOWN)
            dest, tile_expert, n_valid = moe_plan(route, TM_MOE, MOE_ROWS)
            xs = moe_dispatch(hn_packed, dest, MOE_ROWS, TR_MOE)
            a = moe_up(xs, w_moe_gate[j], w_moe_up[j], tile_expert, n_valid, TM_MOE, 512)
            y = moe_down(a, w_moe_down[j], tile_expert, n_valid, TM_MOE, 512)
            h = moe_combine(y, dest, route, h, TR_MOE)

        v_f = proj[:, OFF_DSA_V:OFF_DSA_V + kvw]
        outs['kp'].append(k_f[:N_PROMPT].reshape(BATCH, SEQ, DSA_KV, HEAD_DIM))
        outs['vp'].append(v_f[:N_PROMPT].reshape(BATCH, SEQ, DSA_KV, HEAD_DIM))
        outs['kip'].append(ki_f[:N_PROMPT].reshape(BATCH, SEQ, IDX_DIM))
        outs['sp'].append(s_p)
        outs['ks'].append(k_f[N_PROMPT:].reshape(DEC_BATCH, DEC_SEQ, DSA_KV, HEAD_DIM))
        outs['vs'].append(v_f[N_PROMPT:].reshape(DEC_BATCH, DEC_SEQ, DSA_KV, HEAD_DIM))
        outs['kis'].append(ki_f[N_PROMPT:].reshape(DEC_BATCH, DEC_SEQ, IDX_DIM))
        outs['ss'].append(s_s)

    st = {n: jnp.stack(v) for n, v in outs.items()}
    return (h[:N_PROMPT].reshape(BATCH, SEQ, D_MODEL), h[N_PROMPT:].reshape(DEC_BATCH, DEC_SEQ, D_MODEL),
            st['kp'], st['vp'], st['kip'], st['sp'], st['ks'], st['vs'], st['kis'], st['ss'])
```

```python
import functools

import jax
import jax.numpy as jnp
from jax import lax
from jax.experimental import pallas as pl
from jax.experimental.pallas import tpu as pltpu

F32 = jnp.float32
BF16 = jnp.bfloat16
I32 = jnp.int32

D_MODEL = 2048
BATCH = 2
SEQ = 8192
DEPTH = 2
DEC_BATCH = 32
DEC_SEQ = 16
PAST_LEN = 2048
CHUNK = 64
QBLK = 128
ROPE_THETA = 10000.0
EPS = 1e-6
GLA_HEADS = 4
GLA_DK = 128
GLA_DV = 256
GLA_RANK = 16
GLA_TAU = 16.0
DSA_HEADS = 8
DSA_KV = 2
DSA_GROUP = DSA_HEADS // DSA_KV
HEAD_DIM = 128
IDX_HEADS = 8
IDX_DIM = 64
TOPK_MAX = 256
MIX_A = GLA_HEADS * GLA_DV
MIX_B = DSA_HEADS * HEAD_DIM
D_FF = 5632
N_EXPERTS = 8
TOP_K = 2

N_PROMPT = BATCH * SEQ
N_SAMPLE = DEC_BATCH * DEC_SEQ
N_TOK = N_PROMPT + N_SAMPLE

LANES = 128
VMEM_LIMIT_BYTES = 56 * 1024 * 1024

COL_SIZES = (
    ('gla_q', 512), ('gla_k', 512), ('gla_v', 1024), ('gla_glr', 16), ('gla_r', 1024),
    ('dsa_q', 1024), ('dsa_k', 256), ('dsa_v', 256), ('idx_q', 512), ('idx_k', 64),
    ('idx_w', 8), ('gate_a', 2048), ('gate_b', 2048),
)
PERM_ORDER = ('gla_q', 'gla_k', 'gla_v', 'gla_r', 'dsa_q', 'dsa_k', 'dsa_v', 'idx_q',
              'gate_a', 'gate_b', 'idx_k', 'gla_glr', 'idx_w')
P_COLS = 9728
OFF_GLA_Q, OFF_GLA_K, OFF_GLA_V, OFF_GLA_R = 0, 512, 1024, 2048
OFF_DSA_Q, OFF_DSA_K, OFF_DSA_V, OFF_IDX_Q = 3072, 4096, 4352, 4608
OFF_GATE_A, OFF_GATE_B, OFF_SMALL = 5120, 7168, 9216
SM_IDX_K, SM_GLR, SM_IDX_W = 0, 64, 80

ROUTE_E1, ROUTE_E2, ROUTE_W1, ROUTE_W2 = 0, 1, 2, 3
HI16 = -65536
KEY_NEG_INF = -2139095041
INT_MIN = -2147483648
MASK_BIAS = -1e30
M_INIT = -5e29
Q_SCALE = (HEAD_DIM ** -0.5) * 1.4426950408889634


def _cparams(*sem):
    return pltpu.CompilerParams(dimension_semantics=sem, vmem_limit_bytes=VMEM_LIMIT_BYTES)


def _dot(a, b):
    return jnp.dot(a, b, preferred_element_type=F32)


def _dot_nt(a, b):
    return lax.dot_general(a, b, (((1,), (1,)), ((), ())), preferred_element_type=F32)


def _dot_tn(a, b):
    return lax.dot_general(a, b, (((0,), (0,)), ((), ())), preferred_element_type=F32)


def _split3(a):
    a1 = a.astype(BF16)
    r1 = a - a1.astype(F32)
    a2 = r1.astype(BF16)
    a3 = (r1 - a2.astype(F32)).astype(BF16)
    return a1, a2, a3


def _rmsnorm_kernel(x_ref, g_ref, o_ref):
    x = x_ref[...]
    ms = jnp.mean(x * x, axis=-1, keepdims=True)
    o_ref[...] = ((x * lax.rsqrt(ms + EPS)) * g_ref[...]).astype(BF16)


def rmsnorm_cast(x, g, tm):
    n, d = x.shape
    return pl.pallas_call(
        _rmsnorm_kernel,
        grid=(n // tm,),
        in_specs=[pl.BlockSpec((tm, d), lambda i: (i, 0)), pl.BlockSpec((1, d), lambda i: (0, 0))],
        out_specs=pl.BlockSpec((tm, d), lambda i: (i, 0)),
        out_shape=jax.ShapeDtypeStruct((n, d), BF16),
        compiler_params=_cparams("parallel"),
        name="rmsnorm_cast",
    )(x, g.reshape(1, d))


def _mm_kernel(x_ref, w_ref, o_ref):
    o_ref[...] = _dot(x_ref[...], w_ref[...].astype(BF16))


def _mm_res_kernel(x_ref, w_ref, r_ref, o_ref):
    o_ref[...] = r_ref[...] + _dot(x_ref[...], w_ref[...].astype(BF16))


def matmul(x, w, tm, tn, residual=None, name="matmul", weight_resident=False):
    m, k = x.shape
    n = w.shape[1]
    if weight_resident:
        grid = (n // tn, m // tm)
        rc = lambda a, b: (b, a)
    else:
        grid = (m // tm, n // tn)
        rc = lambda a, b: (a, b)
    in_specs = [pl.BlockSpec((tm, k), lambda a, b: (rc(a, b)[0], 0)),
                pl.BlockSpec((k, tn), lambda a, b: (0, rc(a, b)[1]))]
    args = [x, w]
    body = _mm_kernel
    if residual is not None:
        in_specs.append(pl.BlockSpec((tm, tn), lambda a, b: rc(a, b)))
        args.append(residual)
        body = _mm_res_kernel
    return pl.pallas_call(
        body,
        grid=grid,
        in_specs=in_specs,
        out_specs=pl.BlockSpec((tm, tn), lambda a, b: rc(a, b)),
        out_shape=jax.ShapeDtypeStruct((m, n), F32),
        compiler_params=_cparams("parallel", "parallel"),
        name=name,
    )(*args)


def _merge_kernel(oa_ref, ob_ref, wa_ref, wb_ref, ga_ref, gb_ref, o_ref):
    a = _dot(oa_ref[...], wa_ref[...].astype(BF16))
    b = _dot(ob_ref[...], wb_ref[...].astype(BF16))
    o_ref[...] = (jax.nn.sigmoid(ga_ref[...]) * a + jax.nn.sigmoid(gb_ref[...]) * b).astype(BF16)


def merge_branches(o_a, o_b, w_pa, w_pb, proj, tm, tn):
    m = o_a.shape[0]
    n = w_pa.shape[1]
    ja, jb = OFF_GATE_A // tn, OFF_GATE_B // tn
    return pl.pallas_call(
        _merge_kernel,
        grid=(m // tm, n // tn),
        in_specs=[
            pl.BlockSpec((tm, MIX_A), lambda i, j: (i, 0)),
            pl.BlockSpec((tm, MIX_B), lambda i, j: (i, 0)),
            pl.BlockSpec((MIX_A, tn), lambda i, j: (0, j)),
            pl.BlockSpec((MIX_B, tn), lambda i, j: (0, j)),
            pl.BlockSpec((tm, tn), lambda i, j: (i, ja + j)),
            pl.BlockSpec((tm, tn), lambda i, j: (i, jb + j)),
        ],
        out_specs=pl.BlockSpec((tm, tn), lambda i, j: (i, j)),
        out_shape=jax.ShapeDtypeStruct((m, n), BF16),
        compiler_params=_cparams("parallel", "parallel"),
        name="merge_branches",
    )(o_a, o_b, w_pa, w_pb, proj, proj)


def _swiglu_kernel(x_ref, wg_ref, wu_ref, o_ref):
    x = x_ref[...]
    a = _dot(x, wg_ref[...].astype(BF16))
    b = _dot(x, wu_ref[...].astype(BF16))
    o_ref[...] = (a * jax.nn.sigmoid(a) * b).astype(BF16)


def swiglu_up(x, w_gate, w_up, tm, tn):
    m, k = x.shape
    f = w_gate.shape[1]
    return pl.pallas_call(
        _swiglu_kernel,
        grid=(m // tm, f // tn),
        in_specs=[
            pl.BlockSpec((tm, k), lambda i, j: (i, 0)),
            pl.BlockSpec((k, tn), lambda i, j: (0, j)),
            pl.BlockSpec((k, tn), lambda i, j: (0, j)),
        ],
        out_specs=pl.BlockSpec((tm, tn), lambda i, j: (i, j)),
        out_shape=jax.ShapeDtypeStruct((m, f), BF16),
        compiler_params=_cparams("parallel", "parallel"),
        name="swiglu_up",
    )(x, w_gate, w_up)


def _router_kernel(x_ref, w_ref, o_ref):
    logits = _dot(x_ref[...], w_ref[...].astype(BF16))
    lane = lax.broadcasted_iota(I32, logits.shape, 1)
    logits = jnp.where(lane < N_EXPERTS, logits, -jnp.inf)
    t1 = jnp.max(logits, axis=-1, keepdims=True)
    i1 = jnp.min(jnp.where(logits == t1, lane, LANES), axis=-1, keepdims=True)
    rest = jnp.where(lane == i1, -jnp.inf, logits)
    t2 = jnp.max(rest, axis=-1, keepdims=True)
    i2 = jnp.min(jnp.where(rest == t2, lane, LANES), axis=-1, keepdims=True)
    e2 = jnp.exp(t2 - t1)
    den = 1.0 + e2
    o_ref[...] = jnp.where(lane == ROUTE_E1, i1.astype(F32), 0.0) + jnp.where(lane == ROUTE_E2, i2.astype(F32), 0.0) \
        + jnp.where(lane == ROUTE_W1, 1.0 / den, 0.0) + jnp.where(lane == ROUTE_W2, e2 / den, 0.0)


def moe_router(x, w_router, tm):
    m, k = x.shape
    w_pad = jnp.pad(w_router, ((0, 0), (0, LANES - w_router.shape[1])))
    return pl.pallas_call(
        _router_kernel,
        grid=(m // tm,),
        in_specs=[pl.BlockSpec((tm, k), lambda i: (i, 0)), pl.BlockSpec((k, LANES), lambda i: (0, 0))],
        out_specs=pl.BlockSpec((tm, LANES), lambda i: (i, 0)),
        out_shape=jax.ShapeDtypeStruct((m, LANES), F32),
        compiler_params=_cparams("parallel"),
        name="moe_router",
    )(x, w_pad)


def _rmsnorm_pack_kernel(x_ref, g_ref, o_ref, p_ref):
    x = x_ref[...]
    ms = jnp.mean(x * x, axis=-1, keepdims=True)
    y = ((x * lax.rsqrt(ms + EPS)) * g_ref[...]).astype(BF16)
    o_ref[...] = y
    bits = pltpu.bitcast(y.astype(F32), I32)
    half = x.shape[1] // 2
    p_ref[...] = (bits[:, half:] & HI16) | lax.shift_right_logical(bits[:, :half], 16)


def rmsnorm_pack(x, g, tm):
    n, d = x.shape
    return pl.pallas_call(
        _rmsnorm_pack_kernel,
        grid=(n // tm,),
        in_specs=[pl.BlockSpec((tm, d), lambda i: (i, 0)), pl.BlockSpec((1, d), lambda i: (0, 0))],
        out_specs=[pl.BlockSpec((tm, d), lambda i: (i, 0)), pl.BlockSpec((tm, d // 2), lambda i: (i, 0))],
        out_shape=[jax.ShapeDtypeStruct((n, d), BF16), jax.ShapeDtypeStruct((n, d // 2), I32)],
        compiler_params=_cparams("parallel"),
        name="rmsnorm_pack",
    )(x, g.reshape(1, d))


def _unpack_bf16(p):
    lo = pltpu.bitcast(lax.shift_left(p, 16), F32).astype(BF16)
    hi = pltpu.bitcast(p & HI16, F32).astype(BF16)
    return lo, hi


def moe_plan(route, tm, m_pad):
    n = route.shape[0]
    ef = route[:, ROUTE_E1:ROUTE_E2 + 1].astype(I32).reshape(2 * n)
    onehot = (ef[:, None] == jnp.arange(N_EXPERTS, dtype=I32)[None, :]).astype(I32)
    counts = jnp.sum(onehot, axis=0)
    rank = jnp.sum((jnp.cumsum(onehot, axis=0) - onehot) * onehot, axis=1)
    padded = ((counts + tm - 1) // tm) * tm
    ends = jnp.cumsum(padded)
    dest = jnp.sum(onehot * (ends - padded)[None, :], axis=1) + rank
    tile_start = jnp.arange(m_pad // tm, dtype=I32) * tm
    tile_expert = jnp.minimum(jnp.sum((tile_start[:, None] >= ends[None, :]).astype(I32), axis=1), N_EXPERTS - 1)
    n_valid = (ends[-1] // tm).reshape(1)
    return dest, tile_expert, n_valid


def _dispatch_kernel(dest_ref, x_ref, xs_in_ref, xs_ref, sem):
    del xs_in_ref
    rows = x_ref.shape[0]

    def row_copy(r, s):
        d = dest_ref[0, 0, TOP_K * r + s]
        return pltpu.make_async_copy(x_ref.at[pl.ds(r, 1), :], xs_ref.at[pl.ds(d, 1), :], sem)

    def start(r, c):
        for s in range(TOP_K):
            row_copy(r, s).start()
        return c

    def wait(r, c):
        for s in range(TOP_K):
            row_copy(r, s).wait()
        return c

    lax.fori_loop(0, rows, start, 0)
    lax.fori_loop(0, rows, wait, 0)


def moe_dispatch(xp, dest, m_pad, tr):
    n, w = xp.shape
    return pl.pallas_call(
        _dispatch_kernel,
        grid=(n // tr,),
        in_specs=[
            pl.BlockSpec((1, 1, TOP_K * tr), lambda i: (i, 0, 0), memory_space=pltpu.SMEM),
            pl.BlockSpec((tr, w), lambda i: (i, 0)),
            pl.BlockSpec(memory_space=pl.ANY),
        ],
        out_specs=pl.BlockSpec(memory_space=pl.ANY),
        out_shape=jax.ShapeDtypeStruct((m_pad, w), I32),
        scratch_shapes=[pltpu.SemaphoreType.DMA(())],
        input_output_aliases={2: 0},
        compiler_params=_cparams("arbitrary"),
        name="moe_dispatch",
    )(dest.reshape(n // tr, 1, TOP_K * tr), xp, jnp.zeros((m_pad, w), I32))


def _moe_up_kernel(te_ref, nv_ref, x_ref, wg_ref, wu_ref, o_ref):
    del te_ref
    valid = pl.program_id(1) < nv_ref[0]

    @pl.when(valid)
    def _():
        lo, hi = _unpack_bf16(x_ref[...])
        half = x_ref.shape[1]
        a = _dot(lo, wg_ref[:half, :].astype(BF16)) + _dot(hi, wg_ref[half:, :].astype(BF16))
        b = _dot(lo, wu_ref[:half, :].astype(BF16)) + _dot(hi, wu_ref[half:, :].astype(BF16))
        o_ref[...] = (a * jax.nn.sigmoid(a) * b).astype(BF16)

    @pl.when(jnp.logical_not(valid))
    def _():
        o_ref[...] = jnp.zeros_like(o_ref)


def moe_up(xs, w_gate, w_up, tile_expert, n_valid, tm, tn):
    m = xs.shape[0]
    _, k, f = w_gate.shape
    grid_spec = pltpu.PrefetchScalarGridSpec(
        num_scalar_prefetch=2,
        grid=(f // tn, m // tm),
        in_specs=[
            pl.BlockSpec((tm, k // 2), lambda j, i, te, nv: (i, 0)),
            pl.BlockSpec((None, k, tn), lambda j, i, te, nv: (te[i], 0, j)),
            pl.BlockSpec((None, k, tn), lambda j, i, te, nv: (te[i], 0, j)),
        ],
        out_specs=pl.BlockSpec((tm, tn), lambda j, i, te, nv: (i, j)),
    )
    return pl.pallas_call(
        _moe_up_kernel,
        grid_spec=grid_spec,
        out_shape=jax.ShapeDtypeStruct((m, f), BF16),
        compiler_params=_cparams("parallel", "arbitrary"),
        name="moe_up",
    )(tile_expert, n_valid, xs, w_gate, w_up)


def _moe_down_kernel(te_ref, nv_ref, a_ref, w_ref, o_ref):
    del te_ref
    valid = pl.program_id(1) < nv_ref[0]

    @pl.when(valid)
    def _():
        o_ref[...] = _dot(a_ref[...], w_ref[...].astype(BF16))

    @pl.when(jnp.logical_not(valid))
    def _():
        o_ref[...] = jnp.zeros_like(o_ref)


def moe_down(a, w_down, tile_expert, n_valid, tm, tn):
    m, f = a.shape
    d = w_down.shape[2]
    grid_spec = pltpu.PrefetchScalarGridSpec(
        num_scalar_prefetch=2,
        grid=(d // tn, m // tm),
        in_specs=[
            pl.BlockSpec((tm, f), lambda j, i, te, nv: (i, 0)),
            pl.BlockSpec((None, f, tn), lambda j, i, te, nv: (te[i], 0, j)),
        ],
        out_specs=pl.BlockSpec((tm, tn), lambda j, i, te, nv: (i, j)),
    )
    return pl.pallas_call(
        _moe_down_kernel,
        grid_spec=grid_spec,
        out_shape=jax.ShapeDtypeStruct((m, d), F32),
        compiler_params=_cparams("parallel", "arbitrary"),
        name="moe_down",
    )(tile_expert, n_valid, a, w_down)


def _combine_kernel(dest_ref, y_ref, h_ref, rt_ref, o_ref, buf_ref, sem):
    rows = h_ref.shape[0]

    def row_copy(r, s):
        d = dest_ref[0, 0, TOP_K * r + s]
        return pltpu.make_async_copy(y_ref.at[pl.ds(d, 1), :], buf_ref.at[s, pl.ds(r, 1), :], sem.at[s])

    def start(r, c):
        for s in range(TOP_K):
            row_copy(r, s).start()
        return c

    def wait(r, c):
        for s in range(TOP_K):
            row_copy(r, s).wait()
        return c

    lax.fori_loop(0, rows, start, 0)
    lax.fori_loop(0, rows, wait, 0)
    rt = rt_ref[...]
    o_ref[...] = h_ref[...] + (rt[:, ROUTE_W1:ROUTE_W1 + 1] * buf_ref[0] + rt[:, ROUTE_W2:ROUTE_W2 + 1] * buf_ref[1])


def moe_combine(y, dest, route, h, tr):
    n, d = h.shape
    return pl.pallas_call(
        _combine_kernel,
        grid=(n // tr,),
        in_specs=[
            pl.BlockSpec((1, 1, TOP_K * tr), lambda i: (i, 0, 0), memory_space=pltpu.SMEM),
            pl.BlockSpec(memory_space=pl.ANY),
            pl.BlockSpec((tr, d), lambda i: (i, 0)),
            pl.BlockSpec((tr, LANES), lambda i: (i, 0)),
        ],
        out_specs=pl.BlockSpec((tr, d), lambda i: (i, 0)),
        out_shape=jax.ShapeDtypeStruct((n, d), F32),
        scratch_shapes=[pltpu.VMEM((TOP_K, tr, d), F32), pltpu.SemaphoreType.DMA((TOP_K,))],
        compiler_params=_cparams("arbitrary"),
        name="moe_combine",
    )(dest.reshape(n // tr, 1, TOP_K * tr), y, h, route)


def _prep_kernel(q_ref, k_ref, v_ref, qi_ref, sm_ref, c128_ref, s128_ref, c64_ref, s64_ref,
                 gq_ref, gk_ref, gki_ref,
                 qo_ref, kf_ref, kb_ref, vb_ref, qio_ref, kif_ref, kib_ref, kt_ref, kit_ref):
    c128, s128 = c128_ref[...], s128_ref[...]
    c64, s64 = c64_ref[...], s64_ref[...]
    lane = lax.broadcasted_iota(I32, c64.shape, 1)
    lane_lo = (lane & 63) < 32

    def rope128(x):
        return x * c128 + pltpu.roll(x, 64, 1) * s128

    def rope64(x):
        partner = jnp.where(lane_lo, pltpu.roll(x, 96, 1), pltpu.roll(x, 32, 1))
        return x * c64 + partner * s64

    def norm128(x, g):
        ms = jnp.mean(x * x, axis=-1, keepdims=True)
        return x * lax.rsqrt(ms + EPS) * g

    gq, gk = gq_ref[...], gk_ref[...]
    for h in range(DSA_HEADS):
        sl = slice(h * HEAD_DIM, (h + 1) * HEAD_DIM)
        qo_ref[:, sl] = (rope128(norm128(q_ref[:, sl], gq)) * Q_SCALE).astype(BF16)
    for h in range(DSA_KV):
        sl = slice(h * HEAD_DIM, (h + 1) * HEAD_DIM)
        kr = rope128(norm128(k_ref[:, sl], gk))
        kf_ref[:, sl] = kr
        kb_ref[:, sl] = kr.astype(BF16)
        kt_ref[sl, :] = kr.T.astype(BF16)
    ones = jnp.ones((v_ref.shape[0], HEAD_DIM), BF16)
    for h in range(DSA_KV):
        vb_ref[:, 2 * h * HEAD_DIM:(2 * h + 1) * HEAD_DIM] = v_ref[:, h * HEAD_DIM:(h + 1) * HEAD_DIM].astype(BF16)
        vb_ref[:, (2 * h + 1) * HEAD_DIM:(2 * h + 2) * HEAD_DIM] = ones
    for t in range(IDX_HEADS * IDX_DIM // LANES):
        sl = slice(t * LANES, (t + 1) * LANES)
        qio_ref[:, sl] = rope64(qi_ref[:, sl]).astype(BF16)
    sm = sm_ref[...]
    ms = jnp.sum(jnp.where(lane < IDX_DIM, sm * sm, 0.0), axis=-1, keepdims=True) * (1.0 / IDX_DIM)
    ki = rope64(sm * lax.rsqrt(ms + EPS) * gki_ref[...])
    kif_ref[...] = ki[:, :IDX_DIM]
    kib_ref[...] = ki[:, :IDX_DIM].astype(BF16)
    kit_ref[...] = ki.T[:IDX_DIM, :].astype(BF16)


def dsa_prep(proj, tabs, g_q, g_k, g_ki, tm):
    n = proj.shape[0]
    c128, s128, c64, s64 = tabs
    gki_pad = jnp.pad(g_ki, (0, LANES - IDX_DIM)).reshape(1, LANES)
    row = lambda w, off: pl.BlockSpec((tm, w), lambda i: (i, off // w))
    tab = pl.BlockSpec((tm, LANES), lambda i: (i, 0))
    vec = pl.BlockSpec((1, LANES), lambda i: (0, 0))
    out = lambda w: pl.BlockSpec((tm, w), lambda i: (i, 0))
    kvw = DSA_KV * HEAD_DIM
    return pl.pallas_call(
        _prep_kernel,
        grid=(n // tm,),
        in_specs=[row(MIX_B, OFF_DSA_Q), row(kvw, OFF_DSA_K), row(kvw, OFF_DSA_V),
                  row(IDX_HEADS * IDX_DIM, OFF_IDX_Q), row(LANES, OFF_SMALL),
                  tab, tab, tab, tab, vec, vec, vec],
        out_specs=[out(MIX_B), out(kvw), out(kvw), out(2 * kvw), out(IDX_HEADS * IDX_DIM),
                   out(IDX_DIM), out(IDX_DIM),
                   pl.BlockSpec((kvw, tm), lambda i: (0, i)), pl.BlockSpec((IDX_DIM, tm), lambda i: (0, i))],
        out_shape=[
            jax.ShapeDtypeStruct((n, MIX_B), BF16),
            jax.ShapeDtypeStruct((n, kvw), F32),
            jax.ShapeDtypeStruct((n, kvw), BF16),
            jax.ShapeDtypeStruct((n, 2 * kvw), BF16),
            jax.ShapeDtypeStruct((n, IDX_HEADS * IDX_DIM), BF16),
            jax.ShapeDtypeStruct((n, IDX_DIM), F32),
            jax.ShapeDtypeStruct((n, IDX_DIM), BF16),
            jax.ShapeDtypeStruct((kvw, n), BF16),
            jax.ShapeDtypeStruct((IDX_DIM, n), BF16),
        ],
        compiler_params=_cparams("parallel"),
        name="dsa_prep",
    )(proj, proj, proj, proj, proj, c128, s128, c64, s64,
      g_q.reshape(1, LANES), g_k.reshape(1, LANES), gki_pad)


def rope_tables(pos):
    def tab(half, reps):
        inv = ROPE_THETA ** (-jnp.arange(half, dtype=F32) / half)
        ang = pos.astype(F32)[:, None] * inv[None, :]
        cos, sin = jnp.cos(ang), jnp.sin(ang)
        return jnp.tile(jnp.concatenate([cos, cos], -1), (1, reps)), jnp.tile(jnp.concatenate([-sin, sin], -1), (1, reps))
    c128, s128 = tab(HEAD_DIM // 2, 1)
    c64, s64 = tab(IDX_DIM // 2, 2)
    return c128, s128, c64, s64


def _gla_kernel(q_ref, k_ref, v_ref, r_ref, sm_ref, wg2_ref, bg_ref, gout_ref, s0_ref,
                o_ref, sout_ref, st_ref, *, chunk, n_chunks):
    t = pl.program_id(1)

    @pl.when(t == 0)
    def _():
        for h in range(GLA_HEADS):
            st_ref[h] = s0_ref[h].T

    row = lax.broadcasted_iota(I32, (chunk, chunk), 0)
    col = lax.broadcasted_iota(I32, (chunk, chunk), 1)
    causal = row >= col
    tril = jnp.where(causal, 1.0, 0.0).astype(BF16)
    gout = gout_ref[...]

    heads = range(GLA_HEADS)
    dks = [slice(h * GLA_DK, (h + 1) * GLA_DK) for h in heads]
    dvs = [slice(h * GLA_DV, (h + 1) * GLA_DV) for h in heads]
    ws = [_split3(wg2_ref[h]) for h in heads]
    bgs = [bg_ref[:, dks[h]] for h in heads]
    sts = [st_ref[h] for h in heads]
    for c in range(n_chunks):
        sl = pl.ds(c * chunk, chunk)
        s1, s2, s3 = _split3(sm_ref[sl, :])
        qs = [q_ref[sl, dks[h]] * (GLA_DK ** -0.5) for h in heads]
        ks = [k_ref[sl, dks[h]] for h in heads]
        vs = [v_ref[sl, dvs[h]].astype(BF16) for h in heads]
        rs = [r_ref[sl, dvs[h]] for h in heads]
        zs = [(_dot(s1, w[0]) + (_dot(s1, w[1]) + _dot(s2, w[0]))
               + (_dot(s1, w[2]) + _dot(s2, w[1]) + _dot(s3, w[0]))) + bgs[h] for h, w in enumerate(ws)]
        gs = [_split3((jnp.minimum(z, 0.0) - jnp.log1p(jnp.exp(-jnp.abs(z)))) * (1.0 / GLA_TAU)) for z in zs]
        bs = [_dot(tril, g[0]) + _dot(tril, g[1]) + _dot(tril, g[2]) for g in gs]
        b_last = [b[chunk - 1:chunk, :] for b in bs]
        qe = [(qs[h] * jnp.exp(bs[h])).astype(BF16) for h in heads]
        ke = [(ks[h] * jnp.exp(-bs[h])).astype(BF16) for h in heads]
        kd = [(ks[h] * jnp.exp(b_last[h] - bs[h])).astype(BF16) for h in heads]
        a = [jnp.where(causal, _dot_nt(qe[h], ke[h]), 0.0).astype(BF16) for h in heads]
        o = [_dot(a[h], vs[h]) + _dot_nt(qe[h], sts[h].astype(BF16)) for h in heads]
        sts = [jnp.exp(b_last[h]) * sts[h] + _dot_tn(vs[h], kd[h]) for h in heads]
        for h in heads:
            ms = jnp.mean(o[h] * o[h], axis=-1, keepdims=True)
            on = o[h] * lax.rsqrt(ms + EPS) * gout
            o_ref[sl, dvs[h]] = (rs[h] * jax.nn.sigmoid(rs[h]) * on).astype(BF16)
    for h in heads:
        st_ref[h] = sts[h]

    @pl.when(t == pl.num_programs(1) - 1)
    def _():
        for h in range(GLA_HEADS):
            sout_ref[h] = st_ref[h].T


def gla_scan(proj, wg2_pad, b_g, g_out, s0, row0, seq, chunk, tc):
    nb = s0.shape[0]
    nt = seq // tc
    rb0 = row0 // tc
    hk, hv = GLA_HEADS * GLA_DK, GLA_HEADS * GLA_DV
    rows = lambda w, off: pl.BlockSpec((tc, w), lambda b, t: (rb0 + b * nt + t, off // w))
    state = pl.BlockSpec((None, GLA_HEADS, GLA_DK, GLA_DV), lambda b, t: (b, 0, 0, 0))
    kern = functools.partial(_gla_kernel, chunk=chunk, n_chunks=tc // chunk)
    return pl.pallas_call(
        kern,
        grid=(nb, nt),
        in_specs=[
            rows(hk, OFF_GLA_Q), rows(hk, OFF_GLA_K), rows(hv, OFF_GLA_V), rows(hv, OFF_GLA_R),
            rows(LANES, OFF_SMALL),
            pl.BlockSpec((GLA_HEADS, LANES, GLA_DK), lambda b, t: (0, 0, 0)),
            pl.BlockSpec((1, hk), lambda b, t: (0, 0)),
            pl.BlockSpec((1, GLA_DV), lambda b, t: (0, 0)),
            state,
        ],
        out_specs=[pl.BlockSpec((tc, hv), lambda b, t: (b * nt + t, 0)), state],
        out_shape=[
            jax.ShapeDtypeStruct((nb * seq, MIX_A), BF16),
            jax.ShapeDtypeStruct((nb, GLA_HEADS, GLA_DK, GLA_DV), F32),
        ],
        scratch_shapes=[pltpu.VMEM((GLA_HEADS, GLA_DV, GLA_DK), F32)],
        compiler_params=_cparams("parallel", "arbitrary"),
        name="gla_scan",
    )(proj, proj, proj, proj, proj, wg2_pad, b_g.reshape(1, -1), g_out.reshape(1, -1), s0)


def gla_gate_weight(w_g2):
    w = w_g2.reshape(GLA_RANK, GLA_HEADS, GLA_DK).transpose(1, 0, 2)
    return jnp.pad(w, ((0, 0), (SM_GLR, LANES - SM_GLR - GLA_RANK), (0, 0)))


def _dsa_kernel(q_ref, qi_ref, sm_ref, k_ref, v_ref, ki_ref, o_ref,
                score_scr, vec_scr, qis_scr, qg_scr, m_scr, a_scr, p_scr, acc_scr,
                *, qb, tk, lp, n_sel, l_valid, causal_blocks, qpos_static, keys_t):
    gq = DSA_GROUP * qb
    if causal_blocks:
        i = pl.program_id(1)
        qpos0 = i * qb
        nt = ((i + 1) * qb + tk - 1) // tk
    else:
        qpos0 = qpos_static
        nt = lp // tk
    nsub = tk // LANES
    idx_bits = max(1, (lp - 1).bit_length())

    for h in range(IDX_HEADS):
        qis_scr[h * qb:(h + 1) * qb, :] = qi_ref[:, h * IDX_DIM:(h + 1) * IDX_DIM]
    for g in range(DSA_KV):
        for hh in range(DSA_GROUP):
            hd = g * DSA_GROUP + hh
            qg_scr[g, hh * qb:(hh + 1) * qb, :] = q_ref[:, hd * HEAD_DIM:(hd + 1) * HEAD_DIM]
    wq = sm_ref[:, SM_IDX_W:SM_IDX_W + IDX_HEADS] * ((IDX_DIM ** -0.5) * (IDX_HEADS ** -0.5))
    w_cols = [wq[:, h:h + 1] for h in range(IDX_HEADS)]
    row_chunk = (qpos0 + lax.broadcasted_iota(I32, (qb, tk), 0)) >> 6
    lane_t = lax.broadcasted_iota(I32, (qb, tk), 1)

    def p1(j, carry):
        ks = pl.multiple_of(j * tk, tk)
        if keys_t:
            s_all = _dot(qis_scr[...], ki_ref[:, pl.ds(ks, tk)])
        else:
            s_all = _dot_nt(qis_scr[...], ki_ref[pl.ds(ks, tk), :])
        acc = jnp.zeros((qb, tk), F32)
        for h in range(IDX_HEADS):
            acc = acc + jnp.maximum(s_all[h * qb:(h + 1) * qb, :], 0.0) * w_cols[h]
        kpos = ks + lane_t
        adm = jnp.where((kpos >> 6) <= row_chunk, kpos, l_valid) < l_valid
        score_scr[:, pl.ds(ks, tk)] = jnp.where(adm, acc, -jnp.inf)
        return carry

    lax.fori_loop(0, nt, p1, 0)

    rh = min(qb, 128)

    row_groups = list(range(0, qb, rh))

    def count(pred, *row_vecs):
        for i, v in enumerate(row_vecs):
            vec_scr[i] = jnp.broadcast_to(v, (qb, LANES))

        def body(j, accs):
            accs = list(accs)
            for gi, r0 in enumerate(row_groups):
                vecs = [vec_scr[i, r0:r0 + rh, :] for i in range(len(row_vecs))]
                for c in range(nsub):
                    off = pl.multiple_of(j * tk + c * LANES, LANES)
                    blk = score_scr[r0:r0 + rh, pl.ds(off, LANES)]
                    idx = off + lax.broadcasted_iota(I32, (rh, LANES), 1)
                    accs[gi] = accs[gi] + jnp.where(pred(blk, idx, *vecs), 1, 0)
            return tuple(accs)

        accs = lax.fori_loop(0, nt, body, tuple(jnp.zeros((rh, LANES), I32) for _ in row_groups))
        acc = accs[0] if len(accs) == 1 else jnp.concatenate(accs, axis=0)
        return jnp.sum(acc, axis=1, keepdims=True)

    def key_value(key):
        return pltpu.bitcast(key ^ ((key >> 31) & 0x7FFFFFFF), F32)

    c0 = count(lambda blk, idx: blk >= 0.0)
    thr_key = jnp.where(c0 >= n_sel, 0, INT_MIN).astype(I32)

    def bit_step(it, thr_key):
        cand = thr_key + lax.shift_left(jnp.int32(1), 30 - it)
        cnt = count(lambda blk, idx, c: blk >= c, key_value(cand))
        return jnp.where(cnt >= n_sel, cand, thr_key)

    thr_key = lax.fori_loop(0, 31, bit_step, thr_key)
    bounded = thr_key > KEY_NEG_INF
    thr = jnp.where(bounded, key_value(thr_key), -jnp.inf)
    cnt_ge = count(lambda blk, idx, t: blk >= t, thr)
    tie_rows = jnp.where(bounded, cnt_ge, 0) > n_sel
    any_tie = jnp.max(jnp.where(tie_rows, 1, 0)) > 0

    def tie_cut():
        need = n_sel - count(lambda blk, idx, t: blk > t, thr)
        pos = jnp.zeros((qb, 1), I32)
        for b in range(idx_bits - 1, -1, -1):
            cand = pos + (1 << b)
            f = count(lambda blk, idx, t, c: jnp.where(blk == t, idx.astype(F32), float(lp)) < c,
                      thr, cand.astype(F32))
            pos = jnp.where(f < need, cand, pos)
        return pos

    cut = lax.cond(any_tie, tie_cut, lambda: jnp.full((qb, 1), lp, I32))

    m_scr[...] = jnp.full(m_scr.shape, M_INIT, F32)
    acc_scr[...] = jnp.zeros(acc_scr.shape, F32)
    rb = min(qb, 64)

    def p3(j, carry):
        ks = pl.multiple_of(j * tk, tk)
        score = score_scr[:, pl.ds(ks, tk)]
        idx = ks + lane_t
        eq_ok = jnp.where(score == thr, idx, lp + 1) <= cut
        bias = jnp.where(score > thr, 0.0, jnp.where(eq_ok, 0.0, MASK_BIAS))
        bias = jnp.where(score > -jnp.inf, bias, MASK_BIAS)
        if keys_t:
            s_all = [_dot(qg_scr[g], k_ref[g * HEAD_DIM:(g + 1) * HEAD_DIM, pl.ds(ks, tk)])
                     for g in range(DSA_KV)]
        else:
            s_all = [_dot_nt(qg_scr[g], k_ref[pl.ds(ks, tk), g * HEAD_DIM:(g + 1) * HEAD_DIM])
                     for g in range(DSA_KV)]
        for g in range(DSA_KV):
            for r0 in range(0, gq, rb):
                rs = slice(r0, r0 + rb)
                s = s_all[g][rs, :] + bias[r0 % qb:r0 % qb + rb, :]
                m_old = m_scr[g, rs, :]
                m_new = jnp.maximum(m_old, jnp.max(s, axis=1, keepdims=True))
                p_scr[g, rs, :] = jnp.exp2(s - m_new).astype(BF16)
                a_scr[g, rs, :] = jnp.exp2(m_old - m_new)
                m_scr[g, rs, :] = m_new
        for g in range(DSA_KV):
            vt = v_ref[pl.ds(ks, tk), 2 * g * HEAD_DIM:(2 * g + 2) * HEAD_DIM]
            acc_scr[g] = a_scr[g] * acc_scr[g] + _dot(p_scr[g], vt)
        return carry

    lax.fori_loop(0, nt, p3, 0)

    for g in range(DSA_KV):
        acc = acc_scr[g]
        o = acc[:, :HEAD_DIM] / acc[:, HEAD_DIM:]
        for hh in range(DSA_GROUP):
            hd = g * DSA_GROUP + hh
            o_ref[:, hd * HEAD_DIM:(hd + 1) * HEAD_DIM] = o[hh * qb:(hh + 1) * qb, :].astype(BF16)


def dsa_attention(q, qi, proj, k, v, ki, *, row0, nq, qb, tk, n_sel, l_valid, causal_blocks, qpos_static,
                  keys_t=False):
    nb, lp, _ = v.shape
    kvw = DSA_KV * HEAD_DIM
    rb0 = row0 // qb
    gq = DSA_GROUP * qb
    kern = functools.partial(_dsa_kernel, qb=qb, tk=tk, lp=lp, n_sel=n_sel, l_valid=l_valid,
                             causal_blocks=causal_blocks, qpos_static=qpos_static, keys_t=keys_t)
    if keys_t:
        k_spec = pl.BlockSpec((kvw, lp), lambda b, i: (0, b), pipeline_mode=pl.Buffered(1))
        ki_spec = pl.BlockSpec((IDX_DIM, lp), lambda b, i: (0, b), pipeline_mode=pl.Buffered(1))
    else:
        k_spec = pl.BlockSpec((None, lp, kvw), lambda b, i: (b, 0, 0))
        ki_spec = pl.BlockSpec((None, lp, IDX_DIM), lambda b, i: (b, 0, 0))
    return pl.pallas_call(
        kern,
        grid=(nb, nq),
        in_specs=[
            pl.BlockSpec((qb, MIX_B), lambda b, i: (rb0 + b * nq + i, 0)),
            pl.BlockSpec((qb, IDX_HEADS * IDX_DIM), lambda b, i: (rb0 + b * nq + i, 0)),
            pl.BlockSpec((qb, LANES), lambda b, i: (rb0 + b * nq + i, OFF_SMALL // LANES)),
            k_spec,
            pl.BlockSpec((None, lp, 2 * kvw), lambda b, i: (b, 0, 0), pipeline_mode=pl.Buffered(1)),
            ki_spec,
        ],
        out_specs=pl.BlockSpec((qb, MIX_B), lambda b, i: (b * nq + i, 0)),
        out_shape=jax.ShapeDtypeStruct((nb * nq * qb, MIX_B), BF16),
        scratch_shapes=[
            pltpu.VMEM((qb, lp), F32),
            pltpu.VMEM((2, qb, LANES), F32),
            pltpu.VMEM((IDX_HEADS * qb, IDX_DIM), BF16),
            pltpu.VMEM((DSA_KV, gq, HEAD_DIM), BF16),
            pltpu.VMEM((DSA_KV, gq, 1), F32),
            pltpu.VMEM((DSA_KV, gq, 1), F32),
            pltpu.VMEM((DSA_KV, gq, tk), BF16),
            pltpu.VMEM((DSA_KV, gq, 2 * HEAD_DIM), F32),
        ],
        compiler_params=_cparams("parallel", "arbitrary"),
        name="dsa_attention",
    )(q, qi, proj, k, v, ki)


def permute_w_in(w_in):
    offs, o = {}, 0
    for name, size in COL_SIZES:
        offs[name] = (o, size)
        o += size
    parts = [w_in[:, offs[n][0]:offs[n][0] + offs[n][1]] for n in PERM_ORDER]
    used = sum(offs[n][1] for n in PERM_ORDER)
    parts.append(jnp.zeros((w_in.shape[0], P_COLS - used), w_in.dtype))
    return jnp.concatenate(parts, axis=1)


TM_BIG = 1536
TM_DOWN = 768
TC_GLA = 256
TK_DSA = 1024
QB_DSA = 256
SAMPLE_LP = 2176
TM_MOE = 512
TR_MOE = 256
MOE_ROWS = -(-(TOP_K * N_TOK + N_EXPERTS * (TM_MOE - 1)) // TM_MOE) * TM_MOE


def kernel(x_prompt, x_sample, cache_k, cache_v, cache_ki, state_gla, g_attn, w_in, w_gla_gate2, b_gla_gate,
           g_gla_out, g_q, g_k, g_ki, w_branch_a, w_branch_b, w_out, g_ffn, w_ff_gate, w_ff_up, w_ff_down,
           w_router, w_moe_gate, w_moe_up, w_moe_down):
    h = jnp.concatenate([x_prompt.reshape(N_PROMPT, D_MODEL), x_sample.reshape(N_SAMPLE, D_MODEL)], axis=0)
    pos = jnp.concatenate([jnp.tile(jnp.arange(SEQ, dtype=I32), BATCH),
                           jnp.tile(PAST_LEN + jnp.arange(DEC_SEQ, dtype=I32), DEC_BATCH)])
    tabs = rope_tables(pos)
    kvw = DSA_KV * HEAD_DIM
    l_sample = PAST_LEN + DEC_SEQ
    zeros_state = jnp.zeros((BATCH, GLA_HEADS, GLA_DK, GLA_DV), F32)
    outs = {n: [] for n in ('kp', 'vp', 'kip', 'sp', 'ks', 'vs', 'kis', 'ss')}

    for l in range(DEPTH):
        hn = rmsnorm_cast(h, g_attn[l], TM_DOWN)
        proj = matmul(hn, permute_w_in(w_in[l]), TM_BIG, 512, name="in_proj")
        q_b, k_f, k_b, v_b, qi_b, ki_f, ki_b, k_t, ki_t = dsa_prep(proj, tabs, g_q[l], g_k[l], g_ki[l], 512)

        wg2 = gla_gate_weight(w_gla_gate2[l])
        oa_p, s_p = gla_scan(proj, wg2, b_gla_gate[l], g_gla_out[l], zeros_state, 0, SEQ, CHUNK, TC_GLA)
        oa_s, s_s = gla_scan(proj, wg2, b_gla_gate[l], g_gla_out[l], state_gla[l], N_PROMPT, DEC_SEQ,
                             DEC_SEQ, DEC_SEQ)

        ob_p = dsa_attention(
            q_b, qi_b, proj,
            k_t, v_b[:N_PROMPT].reshape(BATCH, SEQ, 2 * kvw), ki_t,
            row0=0, nq=SEQ // QB_DSA, qb=QB_DSA, tk=TK_DSA, n_sel=min(TOPK_MAX, SEQ // 4), l_valid=SEQ,
            causal_blocks=True, qpos_static=0, keys_t=True)

        def with_cache(cache_parts, new, width):
            c = jnp.concatenate([p.astype(BF16) for p in cache_parts], axis=-1)
            n = new[N_PROMPT:].reshape(DEC_BATCH, DEC_SEQ, width)
            pad = jnp.zeros((DEC_BATCH, SAMPLE_LP - l_sample, width), BF16)
            return jnp.concatenate([c, n, pad], axis=1)

        ck, cv = cache_k[l], cache_v[l]
        ones = jnp.ones((DEC_BATCH, PAST_LEN, HEAD_DIM), BF16)
        ob_s = dsa_attention(
            q_b, qi_b, proj,
            with_cache([ck[:, :, g, :] for g in range(DSA_KV)], k_b, kvw),
            with_cache([p for g in range(DSA_KV) for p in (cv[:, :, g, :], ones)], v_b, 2 * kvw),
            with_cache([cache_ki[l]], ki_b, IDX_DIM),
            row0=N_PROMPT, nq=1, qb=DEC_SEQ, tk=SAMPLE_LP, n_sel=min(TOPK_MAX, l_sample // 4),
            l_valid=l_sample, causal_blocks=False, qpos_static=PAST_LEN)

        o_a = jnp.concatenate([oa_p, oa_s], axis=0)
        o_b = jnp.concatenate([ob_p, ob_s], axis=0)
        merged = merge_branches(o_a, o_b, w_branch_a[l], w_branch_b[l], proj, TM_BIG, 512)
        h = matmul(merged, w_out[l], TM_BIG, 512, residual=h, name="out_proj")

        j = l // 2
        if l % 2 == 0:
            hn = rmsnorm_cast(h, g_ffn[l], TM_DOWN)
            a = swiglu_up(hn, w_ff_gate[j], w_ff_up[j], TM_BIG, 256)
            h = matmul(a, w_ff_down[j], TM_DOWN, 512, residual=h, name="ffn_down", weight_resident=True)
        else:
            hn, hn_packed = rmsnorm_pack(h, g_ffn[l], TM_DOWN)
            route = moe_router(hn, w_router[j], TM_DOWN)
            dest, tile_expert, n_valid = moe_plan(route, TM_MOE, MOE_ROWS)
            xs = moe_dispatch(hn_packed, dest, MOE_ROWS, TR_MOE)
            a = moe_up(xs, w_moe_gate[j], w_moe_up[j], tile_expert, n_valid, TM_MOE, 512)
            y = moe_down(a, w_moe_down[j], tile_expert, n_valid, TM_MOE, 512)
            h = moe_combine(y, dest, route, h, TR_MOE)

        v_f = proj[:, OFF_DSA_V:OFF_DSA_V + kvw]
        outs['kp'].append(k_f[:N_PROMPT].reshape(BATCH, SEQ, DSA_KV, HEAD_DIM))
        outs['vp'].append(v_f[:N_PROMPT].reshape(BATCH, SEQ, DSA_KV, HEAD_DIM))
        outs['kip'].append(ki_f[:N_PROMPT].reshape(BATCH, SEQ, IDX_DIM))
        outs['sp'].append(s_p)
        outs['ks'].append(k_f[N_PROMPT:].reshape(DEC_BATCH, DEC_SEQ, DSA_KV, HEAD_DIM))
        outs['vs'].append(v_f[N_PROMPT:].reshape(DEC_BATCH, DEC_SEQ, DSA_KV, HEAD_DIM))
        outs['kis'].append(ki_f[N_PROMPT:].reshape(DEC_BATCH, DEC_SEQ, IDX_DIM))
        outs['ss'].append(s_s)

    st = {n: jnp.stack(v) for n, v in outs.items()}
    return (h[:N_PROMPT].reshape(BATCH, SEQ, D_MODEL), h[N_PROMPT:].reshape(DEC_BATCH, DEC_SEQ, D_MODEL),
            st['kp'], st['vp'], st['kip'], st['sp'], st['ks'], st['vs'], st['kis'], st['ss'])
```
